```python
import math
import jax, jax.numpy as jnp
from jax import lax
import numpy as np

D_MODEL = 1024
BATCH = 32
SEQ = 2048
DEPTH = 2

N_A_LAYERS = (DEPTH + 1) // 2
N_B_LAYERS = DEPTH - N_A_LAYERS
HEAD_DIM = 64
N_HEADS = D_MODEL // HEAD_DIM
N_KV = 4
GQA_R = N_HEADS // N_KV
KV_DIM = N_KV * HEAD_DIM
NUM_BUCKETS = 32
MAX_DISTANCE = 2048
CMP_BLOCK = 32
CMP_STRIDE = 16
CMP_HIDDEN = 256
SEL_BLOCK = 64
SEL_TOPN = 8
SEL_Q_CHUNK = 16
FORCE_BONUS = 100.0
WIN = 256
BAND_BLOCK = 128
DILATIONS = ((128, 1), (512, 4), (2048, 16))
N_DIL = len(DILATIONS)
FFN_HIDDEN = -(-8 * D_MODEL // (3 * 256)) * 256
RMS_EPS = 1e-6
NEG_INF = -1e30
A_IN_DIM = D_MODEL + 6 * KV_DIM + 3 * N_HEADS
B_Q_DIM = N_DIL * D_MODEL
SHARED_KV_DIM = N_DIL * 2 * KV_DIM

kernel_name = "yoco_nsa_dilated_hybrid"


def rmsnorm(x, g):
    x32 = x.astype(jnp.float32)
    y = x32 * lax.rsqrt(jnp.mean(x32 * x32, axis=-1, keepdims=True) + RMS_EPS)
    return (y * g.astype(jnp.float32)).astype(x.dtype)


def t5_bucket(dist):
    max_exact = NUM_BUCKETS // 2
    d = jnp.maximum(dist, 0)
    log_ratio = jnp.log(jnp.maximum(d, max_exact).astype(jnp.float32) / max_exact) / math.log(MAX_DISTANCE / max_exact)
    large = max_exact + (log_ratio * (NUM_BUCKETS - max_exact)).astype(jnp.int32)
    return jnp.where(d < max_exact, d, jnp.minimum(large, NUM_BUCKETS - 1))


def masked_softmax(s, mask):
    s = jnp.where(mask, s.astype(jnp.float32), NEG_INF)
    m = jnp.max(s, axis=-1, keepdims=True)
    p = jnp.where(mask, jnp.exp(s - m), 0.0)
    return p, jnp.sum(p, axis=-1, keepdims=True), m


def banded_attention(q, k, v, max_dist, dist_scale, rel_bias):
    n, L, G, R, dh = q.shape
    nb = L // BAND_BLOCK
    n_prev = -(-max_dist // BAND_BLOCK)
    pad = n_prev * BAND_BLOCK
    kw_len = (n_prev + 1) * BAND_BLOCK

    def windows(t):
        tp = jnp.pad(t, ((0, 0), (pad, 0), (0, 0), (0, 0))).reshape(n, nb + n_prev, BAND_BLOCK, G, dh)
        return jnp.concatenate([tp[:, j:j + nb] for j in range(n_prev + 1)], axis=2)

    kw, vw = windows(k), windows(v)
    qb = q.reshape(n, nb, BAND_BLOCK, G, R, dh)
    s = jnp.einsum('nbqgrd,nbkgd->nbgrqk', qb, kw).astype(jnp.float32) * dh ** -0.5
    qi = jnp.arange(BAND_BLOCK)[:, None]
    ki = jnp.arange(kw_len)[None, :]
    dist = pad + qi - ki
    key_pos = jnp.arange(nb)[:, None, None] * BAND_BLOCK - pad + ki[None]
    mask = (dist >= 0) & (dist <= max_dist) & (key_pos >= 0)
    bias = rel_bias[t5_bucket(dist * dist_scale)].reshape(BAND_BLOCK, kw_len, G, R).transpose(2, 3, 0, 1)
    p, denom, m = masked_softmax(s + bias, mask[None, :, None, None])
    o = jnp.einsum('nbgrqk,nbkgd->nbqgrd', p, vw.astype(jnp.float32))
    o = o / denom[..., 0].transpose(0, 1, 4, 2, 3)[..., None]
    lse = (jnp.log(denom) + m)[..., 0].transpose(0, 1, 4, 2, 3).reshape(n, L, G, R)
    return o.reshape(n, L, G, R, dh), lse


def dilated_group_attention(q, k, v, dilation, window, rel_bias):
    B, S = q.shape[:2]
    L = S // dilation
    Lp = -(-L // BAND_BLOCK) * BAND_BLOCK

    def to_sub(t):
        rest = t.shape[2:]
        t = t.reshape((B, L, dilation) + rest)
        t = jnp.moveaxis(t, 2, 1).reshape((B * dilation, L) + rest)
        return jnp.pad(t, ((0, 0), (0, Lp - L)) + ((0, 0),) * len(rest))

    def from_sub(t):
        rest = t.shape[2:]
        t = t[:, :L].reshape((B, dilation, L) + rest)
        return jnp.moveaxis(t, 1, 2).reshape((B, S) + rest)

    o, lse = banded_attention(to_sub(q), to_sub(k), to_sub(v), window // dilation, dilation, rel_bias)
    return from_sub(o), from_sub(lse)


def nsa_attention(h, w_in, b_gate, pe_k, w1_k, w2_k, pe_v, w1_v, w2_v, w_out, rel_bias):
    B, S, _ = h.shape
    proj = h @ w_in
    splits = [D_MODEL + i * KV_DIM for i in range(7)]
    q, kc, vc, ks, vs, kwn, vwn, gl = jnp.split(proj, splits, axis=-1)
    q = q.reshape(B, S, N_KV, GQA_R, HEAD_DIM)
    kvh = lambda t: t.reshape(B, S, N_KV, HEAD_DIM)
    scale = HEAD_DIM ** -0.5
    t_pos = jnp.arange(S)

    n_c = (S - CMP_BLOCK) // CMP_STRIDE + 1
    starts = jnp.arange(n_c) * CMP_STRIDE
    tok_idx = starts[:, None] + jnp.arange(CMP_BLOCK)[None, :]

    def compress(t, pe, w1, w2):
        blk = t[:, tok_idx] + pe[:, None, :]
        blk = jnp.moveaxis(blk, 3, 2).reshape(B, n_c, N_KV, CMP_BLOCK * HEAD_DIM)
        return jax.nn.gelu(blk @ w1) @ w2

    k_cmp = compress(kvh(kc), pe_k, w1_k, w2_k)
    v_cmp = compress(kvh(vc), pe_v, w1_v, w2_v)
    dist_c = t_pos[:, None] - (starts + CMP_BLOCK - 1)[None, :]
    bias_c = rel_bias[t5_bucket(dist_c)].reshape(S, n_c, N_KV, GQA_R).transpose(2, 3, 0, 1)
    s = jnp.einsum('bsgrd,bngd->bgrsn', q, k_cmp).astype(jnp.float32) * scale
    p, denom, _ = masked_softmax(s + bias_c, dist_c >= 0)
    p_cmp = p / jnp.maximum(denom, 1e-30)
    o_cmp = jnp.einsum('bgrsn,bngd->bsgrd', p_cmp, v_cmp.astype(jnp.float32))

    n_sel = S // SEL_BLOCK
    top_n = min(SEL_TOPN, n_sel)
    ci = jnp.arange(n_c)[:, None] * CMP_STRIDE
    sj = jnp.arange(n_sel)[None, :] * SEL_BLOCK
    overlap = ((ci < sj + SEL_BLOCK) & (ci + CMP_BLOCK > sj)).astype(jnp.float32)
    imp = jnp.einsum('bgrsn,nj->bgsj', p_cmp, overlap)
    blk_j = jnp.arange(n_sel)[None, :]
    cur = (t_pos // SEL_BLOCK)[:, None]
    forced = (blk_j == 0) | (blk_j == cur) | (blk_j == cur - 1)
    score = jnp.where(forced, imp + FORCE_BONUS, jnp.where(blk_j <= cur, imp, -1.0))
    _, sel_idx = lax.top_k(score, top_n)

    k_blocks = kvh(ks).reshape(B, n_sel, SEL_BLOCK, N_KV, HEAD_DIM).transpose(0, 3, 1, 2, 4)
    v_blocks = kvh(vs).reshape(B, n_sel, SEL_BLOCK, N_KV, HEAD_DIM).transpose(0, 3, 1, 2, 4)
    n_chunk = S // SEL_Q_CHUNK
    q_ch = jnp.moveaxis(q.reshape(B, n_chunk, SEL_Q_CHUNK, N_KV, GQA_R, HEAD_DIM), 1, 0)
    idx_ch = jnp.moveaxis(sel_idx.reshape(B, N_KV, n_chunk, SEL_Q_CHUNK, top_n), 2, 0)
    c0 = jnp.arange(n_chunk) * SEL_Q_CHUNK
    b_ix = jnp.arange(B)[:, None, None, None]
    g_ix = jnp.arange(N_KV)[None, :, None, None]
    rb_g = rel_bias.reshape(NUM_BUCKETS, N_KV, GQA_R).transpose(1, 0, 2)
    n_keys = top_n * SEL_BLOCK

    def sel_chunk(args):
        qc, ic, start = args
        kg = k_blocks[b_ix, g_ix, ic].reshape(B, N_KV, SEL_Q_CHUNK, n_keys, HEAD_DIM)
        vg = v_blocks[b_ix, g_ix, ic].reshape(B, N_KV, SEL_Q_CHUNK, n_keys, HEAD_DIM)
        kpos = (ic[..., None] * SEL_BLOCK + jnp.arange(SEL_BLOCK)).reshape(B, N_KV, SEL_Q_CHUNK, n_keys)
        dist = (start + jnp.arange(SEL_Q_CHUNK))[:, None] - kpos
        bias = jnp.moveaxis(rb_g[g_ix, t5_bucket(dist)], -1, 3)
        s_sel = jnp.einsum('bcgrd,bgckd->bgcrk', qc, kg).astype(jnp.float32) * scale + bias
        p_sel, den, _ = masked_softmax(s_sel, (dist >= 0)[:, :, :, None, :])
        return jnp.einsum('bgcrk,bgckd->bcgrd', p_sel / jnp.maximum(den, 1e-30), vg.astype(jnp.float32))

    o_sel = jnp.moveaxis(lax.map(sel_chunk, (q_ch, idx_ch, c0)), 0, 1).reshape(B, S, N_KV, GQA_R, HEAD_DIM)

    o_win, _ = banded_attention(q, kvh(kwn), kvh(vwn), WIN - 1, 1, rel_bias)

    g = jax.nn.sigmoid((gl + b_gate).astype(jnp.float32)).reshape(B, S, N_KV, GQA_R, 3)
    o = g[..., 0:1] * o_cmp + g[..., 1:2] * o_sel + g[..., 2:3] * o_win
    return o.reshape(B, S, D_MODEL).astype(h.dtype) @ w_out


def dilated_attention(h, w_q, k_shared, v_shared, w_out, rel_bias):
    B, S, _ = h.shape
    q = (h @ w_q).reshape(B, S, N_DIL, N_KV, GQA_R, HEAD_DIM)
    outs, lses = [], []
    for gi, (window, dilation) in enumerate(DILATIONS):
        o, lse = dilated_group_attention(q[:, :, gi], k_shared[:, :, gi], v_shared[:, :, gi], dilation, window, rel_bias)
        outs.append(o)
        lses.append(lse)
    alpha = jax.nn.softmax(jnp.stack(lses, axis=-1), axis=-1)
    o = jnp.einsum('bsgrdi,bsgri->bsgrd', jnp.stack(outs, axis=-1), alpha)
    return o.reshape(B, S, D_MODEL).astype(h.dtype) @ w_out


def swiglu(h, w_up, w_down):
    a, b = jnp.split(h @ w_up, 2, axis=-1)
    return (jax.nn.silu(a) * b) @ w_down


def setup_inputs(seed: int = 0) -> dict:
    key = jax.random.key(seed)
    ks = jax.random.split(key, 20)
    f32 = jnp.float32
    nrm = lambda k, shape, s: s * jax.random.normal(k, shape, f32)
    w = lambda k, shape, fan_in: jax.random.normal(k, shape, f32) * fan_in ** -0.5
    gain = lambda k, shape: 1.0 + 0.05 * jax.random.normal(k, shape, f32)
    cmp_in = CMP_BLOCK * HEAD_DIM
    return {
        "x": jax.random.normal(ks[0], (BATCH, SEQ, D_MODEL), f32),
        "rel_bias": nrm(ks[1], (NUM_BUCKETS, N_HEADS), 0.5),
        "norm_mix": gain(ks[2], (DEPTH, D_MODEL)),
        "norm_ffn": gain(ks[3], (DEPTH, D_MODEL)),
        "a_w_in": w(ks[4], (N_A_LAYERS, D_MODEL, A_IN_DIM), D_MODEL),
        "a_b_gate": nrm(ks[5], (N_A_LAYERS, 3 * N_HEADS), 0.1),
        "a_pe_k": nrm(ks[6], (N_A_LAYERS, CMP_BLOCK, HEAD_DIM), 0.1),
        "a_w1_k": w(ks[7], (N_A_LAYERS, cmp_in, CMP_HIDDEN), cmp_in),
        "a_w2_k": w(ks[8], (N_A_LAYERS, CMP_HIDDEN, HEAD_DIM), CMP_HIDDEN),
        "a_pe_v": nrm(ks[9], (N_A_LAYERS, CMP_BLOCK, HEAD_DIM), 0.1),
        "a_w1_v": w(ks[10], (N_A_LAYERS, cmp_in, CMP_HIDDEN), cmp_in),
        "a_w2_v": w(ks[11], (N_A_LAYERS, CMP_HIDDEN, HEAD_DIM), CMP_HIDDEN),
        "a_w_out": w(ks[12], (N_A_LAYERS, D_MODEL, D_MODEL), D_MODEL),
        "kv_norm": gain(ks[13], (D_MODEL,)),
        "kv_w": w(ks[14], (D_MODEL, SHARED_KV_DIM), D_MODEL),
        "b_w_q": w(ks[15], (N_B_LAYERS, D_MODEL, B_Q_DIM), D_MODEL),
        "b_w_out": w(ks[16], (N_B_LAYERS, D_MODEL, D_MODEL), D_MODEL),
        "ffn_w_up": w(ks[17], (DEPTH, D_MODEL, 2 * FFN_HIDDEN), D_MODEL),
        "ffn_w_down": w(ks[18], (DEPTH, FFN_HIDDEN, D_MODEL), FFN_HIDDEN),
        "final_norm": gain(ks[19], (D_MODEL,)),
    }


def reference(x, rel_bias, norm_mix, norm_ffn, a_w_in, a_b_gate, a_pe_k, a_w1_k, a_w2_k, a_pe_v, a_w1_v, a_w2_v,
              a_w_out, kv_norm, kv_w, b_w_q, b_w_out, ffn_w_up, ffn_w_down, final_norm):
    B, S, _ = x.shape
    k_shared = v_shared = None
    for layer in range(DEPTH):
        h = rmsnorm(x, norm_mix[layer])
        if layer < N_A_LAYERS:
            i = layer
            mix = nsa_attention(h, a_w_in[i], a_b_gate[i], a_pe_k[i], a_w1_k[i], a_w2_k[i],
                                a_pe_v[i], a_w1_v[i], a_w2_v[i], a_w_out[i], rel_bias)
        else:
            if layer == N_A_LAYERS:
                kv = (rmsnorm(x, kv_norm) @ kv_w).reshape(B, S, N_DIL, 2, N_KV, HEAD_DIM)
                k_shared, v_shared = kv[:, :, :, 0], kv[:, :, :, 1]
            j = layer - N_A_LAYERS
            mix = dilated_attention(h, b_w_q[j], k_shared, v_shared, b_w_out[j], rel_bias)
        x = x + mix.astype(x.dtype)
        x = x + swiglu(rmsnorm(x, norm_ffn[layer]), ffn_w_up[layer], ffn_w_down[layer]).astype(x.dtype)
    return rmsnorm(x, final_norm)
```

```python
import functools
import math

import jax
import jax.numpy as jnp
from jax import lax
from jax.experimental import pallas as pl
from jax.experimental.pallas import tpu as pltpu

F32 = jnp.float32
BF16 = jnp.bfloat16

D_MODEL = 1024
HEAD_DIM = 64
N_HEADS = 16
N_KV = 4
GQA_R = 4
KV_DIM = N_KV * HEAD_DIM
NUM_BUCKETS = 32
MAX_DISTANCE = 2048
CMP_BLOCK = 32
CMP_STRIDE = 16
CMP_HIDDEN = 256
SEL_BLOCK = 64
SEL_TOPN = 8
FORCE_BONUS = 100.0
WIN = 256
DILATIONS = ((128, 1), (512, 4), (2048, 16))
FFN_HIDDEN = 2816
RMS_EPS = 1e-6
NEG_INF = -1e30

TQ = 128
ROWS = GQA_R * TQ
TM = 512
VMEM_LIMIT = 56 * 1024 * 1024

NT_DIMS = (((1,), (1,)), ((), ()))


def _cparams(sem):
    return pltpu.CompilerParams(dimension_semantics=sem, vmem_limit_bytes=VMEM_LIMIT)


def _resident(shape):
    nd = len(shape)
    return pl.BlockSpec(shape, lambda *_: (0,) * nd, pipeline_mode=pl.Buffered(1))


def _norm_proj_kernel(x_ref, g_ref, *refs, gain_ids, n_w):
    w_refs, o_refs = refs[:n_w], refs[n_w:]
    x = x_ref[...]
    xn = x * lax.rsqrt(jnp.mean(x * x, axis=-1, keepdims=True) + RMS_EPS)
    hs = {}
    for w_ref, o_ref, gid in zip(w_refs, o_refs, gain_ids):
        if gid not in hs:
            hs[gid] = (xn * g_ref[gid:gid + 1, :]).astype(BF16)
        h = hs[gid]
        n = w_ref.shape[1]
        for c0 in range(0, n, 512):
            c1 = min(n, c0 + 512)
            o_ref[:, c0:c1] = jnp.dot(h, w_ref[:, c0:c1], preferred_element_type=F32).astype(o_ref.dtype)


def _norm_proj(x2, gains, weights, gain_ids, out_dtypes):
    m = x2.shape[0]
    n_w = len(weights)
    kern = functools.partial(_norm_proj_kernel, gain_ids=tuple(gain_ids), n_w=n_w)
    return pl.pallas_call(
        kern,
        grid=(m // TM,),
        in_specs=[pl.BlockSpec((TM, D_MODEL), lambda i: (i, 0)), _resident(gains.shape)]
        + [_resident(w.shape) for w in weights],
        out_specs=[pl.BlockSpec((TM, w.shape[1]), lambda i: (i, 0)) for w in weights],
        out_shape=[jax.ShapeDtypeStruct((m, w.shape[1]), dt) for w, dt in zip(weights, out_dtypes)],
        compiler_params=_cparams(("parallel",)),
        name="norm_proj",
    )(x2, gains, *weights)


def _compress_kernel(tk_ref, tv_ref, pek_ref, pev_ref, w1k_ref, w1v_ref, w2k_ref, w2v_ref, ok_ref, ov_ref):
    half = CMP_STRIDE * HEAD_DIM
    row = lax.broadcasted_iota(jnp.int32, (128, HEAD_DIM), 0)
    for t_ref, pe_ref, w1_ref, w2_ref, o_ref in ((tk_ref, pek_ref, w1k_ref, w2k_ref, ok_ref),
                                                 (tv_ref, pev_ref, w1v_ref, w2v_ref, ov_ref)):
        t = t_ref[0, 0]
        u = jnp.dot(t, w1_ref[:half, :], preferred_element_type=F32)
        v = jnp.dot(t, w1_ref[half:, :], preferred_element_type=F32)
        c = jnp.dot(pe_ref[...].astype(BF16), w1_ref[...], preferred_element_type=F32)[0:1, :]
        hid = u + pltpu.roll(v, shift=127, axis=0) + c
        out = jnp.dot(jax.nn.gelu(hid).astype(BF16), w2_ref[...], preferred_element_type=F32)
        o_ref[0, 0] = jnp.where(row < 127, out, 0.0).astype(o_ref.dtype)


def _compress(t16k, t16v, pek, pev, w1k, w1v, w2k, w2v):
    b = t16k.shape[0]
    tspec = pl.BlockSpec((1, 1, 128, 1024), lambda i, g: (i, g, 0, 0))
    ospec = pl.BlockSpec((1, 1, 128, HEAD_DIM), lambda i, g: (i, g, 0, 0))
    oshape = jax.ShapeDtypeStruct((b, N_KV, 128, HEAD_DIM), BF16)
    return pl.pallas_call(
        _compress_kernel,
        grid=(b, N_KV),
        in_specs=[tspec, tspec, _resident(pek.shape), _resident(pev.shape), _resident(w1k.shape),
                  _resident(w1v.shape), _resident(w2k.shape), _resident(w2v.shape)],
        out_specs=[ospec, ospec],
        out_shape=[oshape, oshape],
        compiler_params=_cparams(("parallel", "parallel")),
        name="nsa_compress",
    )(t16k, t16v, pek, pev, w1k, w1v, w2k, w2v)


def _split3(x):
    hi = x.astype(BF16)
    r1 = x - hi.astype(F32)
    mid = r1.astype(BF16)
    lo = (r1 - mid.astype(F32)).astype(BF16)
    return hi, mid, lo


def _nsa_kernel(q_ref, ks_ref, vs_ref, kw_ref, vw_ref, kc_ref, vc_ref, gl_ref, bg_ref, biasc_ref, tsel_ref,
                twin_ref, ovt_ref, e_ref, o_ref, s_scr, mb_scr):
    i = pl.program_id(2)
    q = q_ref[0].reshape(ROWS, HEAD_DIM)

    bias_c = biasc_ref[0, 0]
    s = lax.dot_general(q, kc_ref[0, 0], NT_DIMS, preferred_element_type=F32) + bias_c
    valid = bias_c > 0.5 * NEG_INF
    m = jnp.max(s, axis=-1, keepdims=True)
    p = jnp.where(valid, jnp.exp(s - m), 0.0)
    den = jnp.sum(p, axis=-1, keepdims=True)
    p_cmp = p / jnp.maximum(den, 1e-30)
    o_cmp = jnp.dot(p_cmp.astype(BF16), vc_ref[0, 0], preferred_element_type=F32)

    psum = p_cmp[0:TQ] + p_cmp[TQ:2 * TQ] + p_cmp[2 * TQ:3 * TQ] + p_cmp[3 * TQ:4 * TQ]
    ovt = ovt_ref[...]
    imp_t = sum(lax.dot_general(ovt, part, NT_DIMS, preferred_element_type=F32) for part in _split3(psum))
    blk = lax.broadcasted_iota(jnp.int32, (32, TQ), 0)
    tok = i * TQ + lax.broadcasted_iota(jnp.int32, (32, TQ), 1)
    cur = lax.shift_right_logical(tok, 6)
    forced = (blk == 0) | (blk == cur) | (blk == cur - 1)
    score = jnp.where(forced, imp_t + FORCE_BONUS, jnp.where(blk <= cur, imp_t, -1.0))
    sel = blk < 0
    for _ in range(SEL_TOPN):
        mx = jnp.max(score, axis=0, keepdims=True)
        idx = jnp.min(jnp.where(score == mx, blk, 32), axis=0, keepdims=True)
        pick = blk == idx
        sel = sel | pick
        score = jnp.where(pick, -3e38, score)
    selneg_t = jnp.where(sel, 0.0, NEG_INF)
    selneg = jnp.concatenate([selneg_t, jnp.zeros((TQ - 32, TQ), F32)], axis=0).T.astype(BF16)
    for c in range(16):
        mb_scr[c] = jnp.dot(selneg, e_ref[c], preferred_element_type=F32)

    def sel_pass1(c, macc):
        r0 = pl.multiple_of(c * 128, 128)
        sc = lax.dot_general(q, ks_ref[0, 0, pl.ds(r0, 128), :], NT_DIMS, preferred_element_type=F32)
        mb = mb_scr[c]
        sc = sc + tsel_ref[0, i - c] + jnp.concatenate([mb, mb, mb, mb], axis=0)
        s_scr[c] = sc
        return jnp.maximum(macc, sc)

    macc = lax.fori_loop(0, i + 1, sel_pass1, jnp.full((ROWS, 128), NEG_INF, F32))
    m_sel = jnp.max(macc, axis=-1, keepdims=True)

    def sel_pass2(c, carry):
        acc, lacc = carry
        r0 = pl.multiple_of(c * 128, 128)
        pc = jnp.exp(s_scr[c] - m_sel)
        acc = acc + jnp.dot(pc.astype(BF16), vs_ref[0, 0, pl.ds(r0, 128), :], preferred_element_type=F32)
        return acc, lacc + pc

    acc, lacc = lax.fori_loop(0, i + 1, sel_pass2,
                              (jnp.zeros((ROWS, HEAD_DIM), F32), jnp.zeros((ROWS, 128), F32)))
    l_sel = jnp.sum(lacc, axis=-1, keepdims=True)
    o_sel = acc / jnp.maximum(l_sel, 1e-30)

    sw = []
    for cc in range(3):
        c = i - 2 + cc
        r0 = pl.multiple_of(jnp.maximum(c, 0) * 128, 128)
        sc = lax.dot_general(q, kw_ref[0, 0, pl.ds(r0, 128), :], NT_DIMS, preferred_element_type=F32)
        sw.append((sc + twin_ref[0, cc] + jnp.where(c >= 0, 0.0, NEG_INF), r0))
    m_w = jnp.max(jnp.maximum(jnp.maximum(sw[0][0], sw[1][0]), sw[2][0]), axis=-1, keepdims=True)
    acc_w = jnp.zeros((ROWS, HEAD_DIM), F32)
    l_w = jnp.zeros((ROWS, 128), F32)
    for sc, r0 in sw:
        pc = jnp.exp(sc - m_w)
        acc_w = acc_w + jnp.dot(pc.astype(BF16), vw_ref[0, 0, pl.ds(r0, 128), :], preferred_element_type=F32)
        l_w = l_w + pc
    o_win = acc_w / jnp.sum(l_w, axis=-1, keepdims=True)

    gates = jax.nn.sigmoid(gl_ref[0, 0] + bg_ref[0])
    for r in range(GQA_R):
        rows = slice(r * TQ, (r + 1) * TQ)
        o_r = (gates[:, 3 * r:3 * r + 1] * o_cmp[rows] + gates[:, 3 * r + 1:3 * r + 2] * o_sel[rows]
               + gates[:, 3 * r + 2:3 * r + 3] * o_win[rows])
        o_ref[0, r] = o_r.astype(o_ref.dtype)


def _nsa_attention(qh, ksh, vsh, kwh, vwh, kcmp, vcmp, glh, bgh, biasc, tsel, twin, ovt, emat):
    b, _, s, _ = qh.shape
    nq = s // TQ
    kv_spec = pl.BlockSpec((1, 1, s, HEAD_DIM), lambda g, bi, i: (bi, g, 0, 0))
    cmp_spec = pl.BlockSpec((1, 1, 128, HEAD_DIM), lambda g, bi, i: (bi, g, 0, 0))
    return pl.pallas_call(
        _nsa_kernel,
        grid=(N_KV, b, nq),
        in_specs=[
            pl.BlockSpec((1, GQA_R, TQ, HEAD_DIM), lambda g, bi, i: (bi, g, i, 0)),
            kv_spec, kv_spec, kv_spec, kv_spec, cmp_spec, cmp_spec,
            pl.BlockSpec((1, 1, TQ, 3 * GQA_R), lambda g, bi, i: (bi, g, i, 0)),
            pl.BlockSpec((1, 1, 3 * GQA_R), lambda g, bi, i: (g, 0, 0)),
            pl.BlockSpec((1, 1, ROWS, 128), lambda g, bi, i: (g, i, 0, 0)),
            pl.BlockSpec((1, nq, ROWS, 128), lambda g, bi, i: (g, 0, 0, 0)),
            pl.BlockSpec((1, 3, ROWS, 128), lambda g, bi, i: (g, 0, 0, 0)),
            _resident(ovt.shape), _resident(emat.shape),
        ],
        out_specs=pl.BlockSpec((1, GQA_R, TQ, HEAD_DIM), lambda g, bi, i: (bi, g, i, 0)),
        out_shape=jax.ShapeDtypeStruct((b, N_HEADS, s, HEAD_DIM), BF16),
        scratch_shapes=[pltpu.VMEM((nq, ROWS, 128), F32), pltpu.VMEM((nq, TQ, 128), F32)],
        compiler_params=_cparams(("parallel", "parallel", "arbitrary")),
        name="nsa_attention",
    )(qh, ksh, vsh, kwh, vwh, kcmp, vcmp, glh, bgh, biasc, tsel, twin, ovt, emat)


def _dil_kernel(q_ref, k_ref, v_ref, tb_ref, o_ref, lse_ref, *, nblk):
    n_it = q_ref.shape[1] * nblk

    def body(it, carry):
        j = it // nblk
        i = it % nblk
        r0 = pl.multiple_of(i * 128, 128)
        q = q_ref[0, j, :, pl.ds(r0, 128), :].reshape(ROWS, HEAD_DIM)
        s1 = lax.dot_general(q, k_ref[0, j, 0, pl.ds(r0, 128), :], NT_DIMS,
                             preferred_element_type=F32) + tb_ref[0, 1]
        if nblk > 1:
            rp = pl.multiple_of(jnp.maximum(i - 1, 0) * 128, 128)
            s0 = lax.dot_general(q, k_ref[0, j, 0, pl.ds(rp, 128), :], NT_DIMS,
                                 preferred_element_type=F32) + tb_ref[0, 0] + jnp.where(i > 0, 0.0, NEG_INF)
            m = jnp.max(jnp.maximum(s0, s1), axis=-1, keepdims=True)
            p0 = jnp.exp(s0 - m)
            p1 = jnp.exp(s1 - m)
            l = jnp.sum(p0 + p1, axis=-1, keepdims=True)
            acc = (jnp.dot(p0.astype(BF16), v_ref[0, j, 0, pl.ds(rp, 128), :], preferred_element_type=F32)
                   + jnp.dot(p1.astype(BF16), v_ref[0, j, 0, pl.ds(r0, 128), :], preferred_element_type=F32))
        else:
            m = jnp.max(s1, axis=-1, keepdims=True)
            p1 = jnp.exp(s1 - m)
            l = jnp.sum(p1, axis=-1, keepdims=True)
            acc = jnp.dot(p1.astype(BF16), v_ref[0, j, 0, pl.ds(r0, 128), :], preferred_element_type=F32)
        o = acc / l
        lse = jnp.log(l) + m
        o_ref[0, j, :, pl.ds(r0, 128), :] = o.reshape(GQA_R, TQ, HEAD_DIM).astype(o_ref.dtype)
        lse_ref[0, j, 0, pl.ds(r0, 128), :] = jnp.concatenate(
            [lse[r * TQ:(r + 1) * TQ] for r in range(GQA_R)], axis=1)
        return carry

    lax.fori_loop(0, n_it, body, 0)


def _dil_attention(qd, kd, vd, tb):
    b, d, _, l, _ = qd.shape
    nblk = l // 128
    q_spec = pl.BlockSpec((1, d, GQA_R, l, HEAD_DIM), lambda bi, g: (bi, 0, g, 0, 0))
    kv_spec = pl.BlockSpec((1, d, 1, l, HEAD_DIM), lambda bi, g: (bi, 0, g, 0, 0))
    return pl.pallas_call(
        functools.partial(_dil_kernel, nblk=nblk),
        grid=(b, N_KV),
        in_specs=[q_spec, kv_spec, kv_spec, pl.BlockSpec((1, 2, ROWS, 128), lambda bi, g: (g, 0, 0, 0))],
        out_specs=[q_spec, pl.BlockSpec((1, d, 1, l, GQA_R), lambda bi, g: (bi, 0, g, 0, 0))],
        out_shape=[jax.ShapeDtypeStruct(qd.shape, BF16), jax.ShapeDtypeStruct((b, d, N_KV, l, GQA_R), F32)],
        compiler_params=_cparams(("parallel", "parallel")),
        name="dilated_attention",
    )(qd, kd, vd, tb)


def _post_kernel(*refs, n_grp, final):
    x_ref = refs[0]
    a_refs = refs[1:1 + n_grp]
    pos = 1 + n_grp
    if n_grp > 1:
        lse_refs = refs[pos:pos + n_grp]
        ex_ref = refs[pos + n_grp]
        pos += n_grp + 1
    wo_ref, gf_ref, wup_ref, wdn_ref = refs[pos:pos + 4]
    pos += 4
    if final:
        gfin_ref = refs[pos]
        pos += 1
    o_ref = refs[pos]

    if n_grp == 1:
        a = a_refs[0][...]
    else:
        lses = [r[...] for r in lse_refs]
        mx = functools.reduce(jnp.maximum, lses)
        ws = [jnp.exp(v - mx) for v in lses]
        tot = functools.reduce(lambda u, v: u + v, ws)
        ex = ex_ref[...]
        a32 = jnp.zeros((TM, D_MODEL), F32)
        for w, a_ref in zip(ws, a_refs):
            alpha = w / tot
            hi, mid, _ = _split3(alpha)
            alpha_x = (jnp.dot(hi, ex, preferred_element_type=F32) + jnp.dot(mid, ex, preferred_element_type=F32))
            a32 = a32 + alpha_x * a_ref[...].astype(F32)
        a = a32.astype(BF16)
    x1 = x_ref[...] + jnp.dot(a, wo_ref[...], preferred_element_type=F32)
    h = (x1 * lax.rsqrt(jnp.mean(x1 * x1, axis=-1, keepdims=True) + RMS_EPS) * gf_ref[...]).astype(BF16)
    acts = []
    ch = 256
    for c0 in range(0, FFN_HIDDEN, ch):
        u = jnp.dot(h, wup_ref[:, c0:c0 + ch], preferred_element_type=F32)
        v = jnp.dot(h, wup_ref[:, FFN_HIDDEN + c0:FFN_HIDDEN + c0 + ch], preferred_element_type=F32)
        acts.append((jax.nn.silu(u) * v).astype(BF16))
    act = jnp.concatenate(acts, axis=1)
    x2 = x1 + jnp.dot(act, wdn_ref[...], preferred_element_type=F32)
    if final:
        x2 = x2 * lax.rsqrt(jnp.mean(x2 * x2, axis=-1, keepdims=True) + RMS_EPS) * gfin_ref[...]
    o_ref[...] = x2


def _post(x2d, attn, lses, ex, wo, gf, wup, wdn, gfin):
    m = x2d.shape[0]
    n_grp = len(attn)
    final = gfin is not None
    row = lambda n: pl.BlockSpec((TM, n), lambda i: (i, 0))
    args = [x2d] + list(attn)
    specs = [row(D_MODEL)] + [row(D_MODEL)] * n_grp
    if n_grp > 1:
        args += list(lses) + [ex]
        specs += [row(N_HEADS)] * n_grp + [_resident(ex.shape)]
    args += [wo, gf, wup, wdn]
    specs += [_resident(wo.shape), _resident(gf.shape), _resident(wup.shape), _resident(wdn.shape)]
    if final:
        args.append(gfin)
        specs.append(_resident(gfin.shape))
    return pl.pallas_call(
        functools.partial(_post_kernel, n_grp=n_grp, final=final),
        grid=(m // TM,),
        in_specs=specs,
        out_specs=row(D_MODEL),
        out_shape=jax.ShapeDtypeStruct((m, D_MODEL), F32),
        compiler_params=_cparams(("parallel",)),
        name="outproj_ffn",
    )(*args)


def _t5_bucket(dist):
    max_exact = NUM_BUCKETS // 2
    d = jnp.maximum(dist, 0)
    log_ratio = jnp.log(jnp.maximum(d, max_exact).astype(F32) / max_exact) / math.log(MAX_DISTANCE / max_exact)
    large = max_exact + (log_ratio * (NUM_BUCKETS - max_exact)).astype(jnp.int32)
    return jnp.where(d < max_exact, d, jnp.minimum(large, NUM_BUCKETS - 1))


def _group_rows(t):
    lead = t.shape[:-3]
    t = t.reshape(lead + (N_KV, GQA_R * TQ, 128))
    return jnp.moveaxis(t, -3, 0)


def _toeplitz(rel_bias, dist, valid, scale=1):
    bias = rel_bias[_t5_bucket(jnp.maximum(dist, 0) * scale)]
    bias = jnp.where(valid[..., None], bias, NEG_INF)
    return _group_rows(jnp.moveaxis(bias, -1, -3)).astype(F32)


def kernel(x, rel_bias, norm_mix, norm_ffn, a_w_in, a_b_gate, a_pe_k, a_w1_k, a_w2_k, a_pe_v, a_w1_v, a_w2_v,
           a_w_out, kv_norm, kv_w, b_w_q, b_w_out, ffn_w_up, ffn_w_down, final_norm):
    b, s, _ = x.shape
    m = b * s
    nq = s // TQ
    scale = HEAD_DIM ** -0.5
    x2d = x.reshape(m, D_MODEL)
    qi = jnp.arange(TQ)[:, None]
    ki = jnp.arange(128)[None, :]

    w_in = a_w_in[0]
    n_gl = 3 * N_HEADS
    w_q = (w_in[:, :D_MODEL] * scale).astype(BF16)
    w_kvc = w_in[:, D_MODEL:D_MODEL + 2 * KV_DIM].astype(BF16)
    w_ksw = w_in[:, D_MODEL + 2 * KV_DIM:D_MODEL + 6 * KV_DIM].astype(BF16)
    w_gl = jnp.pad(w_in[:, D_MODEL + 6 * KV_DIM:], ((0, 0), (0, 128 - n_gl))).astype(BF16)
    q, kvc, ksw, gl = _norm_proj(x2d, norm_mix[0:1], [w_q, w_kvc, w_ksw, w_gl], [0, 0, 0, 0],
                                 [BF16, BF16, BF16, F32])

    def t16(t):
        t = t.reshape(b, s // CMP_STRIDE, CMP_STRIDE, N_KV, HEAD_DIM)
        return t.transpose(0, 3, 1, 2, 4).reshape(b, N_KV, s // CMP_STRIDE, CMP_STRIDE * HEAD_DIM)

    kcmp, vcmp = _compress(
        t16(kvc[:, :KV_DIM]), t16(kvc[:, KV_DIM:]),
        jnp.pad(a_pe_k[0].reshape(1, -1), ((0, 7), (0, 0))), jnp.pad(a_pe_v[0].reshape(1, -1), ((0, 7), (0, 0))),
        a_w1_k[0].astype(BF16), a_w1_v[0].astype(BF16), a_w2_k[0].astype(BF16), a_w2_v[0].astype(BF16))

    def heads(t, n):
        return t.reshape(b, s, n, HEAD_DIM).transpose(0, 2, 1, 3)

    qh = heads(q, N_HEADS)
    ksh, vsh, kwh, vwh = (heads(ksw[:, j * KV_DIM:(j + 1) * KV_DIM], N_KV) for j in range(4))
    glh = gl[:, :n_gl].reshape(b, s, N_KV, 3 * GQA_R).transpose(0, 2, 1, 3)
    bgh = a_b_gate[0].reshape(N_KV, 1, 3 * GQA_R)

    tt = jnp.arange(s).reshape(nq, TQ, 1)
    nn = jnp.arange(128)[None, None, :]
    dist_c = tt - (nn * CMP_STRIDE + CMP_BLOCK - 1)
    biasc = _toeplitz(rel_bias, dist_c, (dist_c >= 0) & (nn < 127))
    dist_s = jnp.arange(nq)[:, None, None] * 128 + qi - ki
    tsel = _toeplitz(rel_bias, dist_s, dist_s >= 0)
    dist_w = (2 - jnp.arange(3))[:, None, None] * 128 + qi - ki
    twin = _toeplitz(rel_bias, dist_w, (dist_w >= 0) & (dist_w <= WIN - 1))
    ci = jnp.arange(128)[None, :] * CMP_STRIDE
    sj = jnp.arange(s // SEL_BLOCK)[:, None] * SEL_BLOCK
    ovt = ((ci < sj + SEL_BLOCK) & (ci + CMP_BLOCK > sj) & (jnp.arange(128)[None, :] < 127)).astype(BF16)
    emat = (jnp.arange(128)[None, :, None]
            == 2 * jnp.arange(nq)[:, None, None] + jnp.arange(128)[None, None, :] // SEL_BLOCK).astype(BF16)

    oh = _nsa_attention(qh, ksh, vsh, kwh, vwh, kcmp, vcmp, glh, bgh, biasc, tsel, twin, ovt, emat)
    attn = oh.transpose(0, 2, 1, 3).reshape(m, D_MODEL)
    x2d = _post(x2d, [attn], None, None, a_w_out[0].astype(BF16), norm_ffn[0:1],
                ffn_w_up[0].astype(BF16), ffn_w_down[0].astype(BF16), None)

    gains = jnp.stack([norm_mix[1], kv_norm])
    q3, kv = _norm_proj(x2d, gains, [(b_w_q[0] * scale).astype(BF16), kv_w.astype(BF16)], [0, 1], [BF16, BF16])
    outs, lses = [], []
    for gi, (window, d) in enumerate(DILATIONS):
        l = s // d
        qd = q3[:, gi * D_MODEL:(gi + 1) * D_MODEL].reshape(b, l, d, N_HEADS, HEAD_DIM).transpose(0, 2, 3, 1, 4)
        kd = kv[:, gi * 2 * KV_DIM:gi * 2 * KV_DIM + KV_DIM].reshape(b, l, d, N_KV, HEAD_DIM).transpose(0, 2, 3, 1, 4)
        vd = kv[:, gi * 2 * KV_DIM + KV_DIM:(gi + 1) * 2 * KV_DIM].reshape(b, l, d, N_KV, HEAD_DIM).transpose(0, 2, 3, 1, 4)
        span = window // d
        dist_d = (1 - jnp.arange(2))[:, None, None] * 128 + qi - ki
        tb = _toeplitz(rel_bias, dist_d, (dist_d >= 0) & (dist_d <= span), scale=d)
        od, lse = _dil_attention(qd, kd, vd, tb)
        outs.append(od.transpose(0, 3, 1, 2, 4).reshape(m, D_MODEL))
        lses.append(lse.transpose(0, 3, 1, 2, 4).reshape(m, N_HEADS))
    ex = (jnp.arange(D_MODEL)[None, :] // HEAD_DIM == jnp.arange(N_HEADS)[:, None]).astype(BF16)
    out = _post(x2d, outs, lses, ex, b_w_out[0].astype(BF16), norm_ffn[1:2],
                ffn_w_up[1].astype(BF16), ffn_w_down[1].astype(BF16), final_norm.reshape(1, D_MODEL))
    return out.reshape(b, s, D_MODEL)
```

```python
import functools
import math

import jax
import jax.numpy as jnp
from jax import lax
from jax.experimental import pallas as pl
from jax.experimental.pallas import tpu as pltpu

F32 = jnp.float32
BF16 = jnp.bfloat16

D_MODEL = 1024
HEAD_DIM = 64
N_HEADS = 16
N_KV = 4
GQA_R = 4
KV_DIM = N_KV * HEAD_DIM
NUM_BUCKETS = 32
MAX_DISTANCE = 2048
CMP_BLOCK = 32
CMP_STRIDE = 16
SEL_BLOCK = 64
SEL_TOPN = 8
FORCE_BONUS = 100.0
WIN = 256
DILATIONS = ((128, 1), (512, 4), (2048, 16))
FFN_HIDDEN = 2816
RMS_EPS = 1e-6
NEG_INF = -1e30

LANES = 128
TQ = 128
HALF = 2 * TQ
ROWS = GQA_R * TQ
HEAD_ORDER = (0, 2, 1, 3)
SUPER = 4
GL_PAD = 16
TM = 512
VMEM_LIMIT = 56 * 1024 * 1024

NT_DIMS = (((1,), (1,)), ((), ()))


def _cparams(sem):
    return pltpu.CompilerParams(dimension_semantics=sem, vmem_limit_bytes=VMEM_LIMIT)


def _resident(shape):
    nd = len(shape)
    return pl.BlockSpec(shape, lambda *_: (0,) * nd, pipeline_mode=pl.Buffered(1))


def _nt(a, b):
    return lax.dot_general(a, b, NT_DIMS, preferred_element_type=F32)


def _dot(a, b):
    return jnp.dot(a, b, preferred_element_type=F32)


def _norm_proj_kernel(x_ref, g_ref, *refs, gain_ids, n_w):
    w_refs, o_refs = refs[:n_w], refs[n_w:]
    x = x_ref[...]
    xn = x * lax.rsqrt(jnp.mean(x * x, axis=-1, keepdims=True) + RMS_EPS)
    hs = {}
    for w_ref, o_ref, gid in zip(w_refs, o_refs, gain_ids):
        if gid not in hs:
            hs[gid] = (xn * g_ref[gid:gid + 1, :]).astype(BF16)
        h = hs[gid]
        n = w_ref.shape[1]
        if len(o_ref.shape) == 3:
            y = _dot(h, w_ref[...])
            w = o_ref.shape[2]
            for g in range(o_ref.shape[0]):
                o_ref[g] = y[:, g * w:(g + 1) * w].astype(o_ref.dtype)
            continue
        for c0 in range(0, n, 512):
            c1 = min(n, c0 + 512)
            o_ref[:, c0:c1] = _dot(h, w_ref[:, c0:c1]).astype(o_ref.dtype)


def _norm_proj(x2, gains, weights, gain_ids, out_shapes, out_dtypes):
    m = x2.shape[0]
    n_w = len(weights)
    kern = functools.partial(_norm_proj_kernel, gain_ids=tuple(gain_ids), n_w=n_w)
    out_specs = []
    for shp in out_shapes:
        if len(shp) == 3:
            out_specs.append(pl.BlockSpec((shp[0], TM, shp[2]), lambda i: (0, i, 0)))
        else:
            out_specs.append(pl.BlockSpec((TM, shp[1]), lambda i: (i, 0)))
    return pl.pallas_call(
        kern,
        grid=(m // TM,),
        in_specs=[pl.BlockSpec((TM, D_MODEL), lambda i: (i, 0)), _resident(gains.shape)]
        + [_resident(w.shape) for w in weights],
        out_specs=out_specs,
        out_shape=[jax.ShapeDtypeStruct(shp, dt) for shp, dt in zip(out_shapes, out_dtypes)],
        compiler_params=_cparams(("parallel",)),
        name="norm_proj",
    )(x2, gains, *weights)


def _compress_kernel(tk_ref, tv_ref, pek_ref, pev_ref, w1k_ref, w1v_ref, w2k_ref, w2v_ref, ok_ref, ov_ref):
    half = CMP_STRIDE * HEAD_DIM
    row = lax.broadcasted_iota(jnp.int32, (128, LANES), 0)
    for t_ref, pe_ref, w1_ref, w2_ref, o_ref in ((tk_ref, pek_ref, w1k_ref, w2k_ref, ok_ref),
                                                 (tv_ref, pev_ref, w1v_ref, w2v_ref, ov_ref)):
        t = t_ref[0, 0]
        u = _dot(t, w1_ref[:half, :])
        v = _dot(t, w1_ref[half:, :])
        c = _dot(pe_ref[...].astype(BF16), w1_ref[...])[0:1, :]
        hid = u + pltpu.roll(v, shift=127, axis=0) + c
        out = _dot(jax.nn.gelu(hid).astype(BF16), w2_ref[...])
        out = jnp.concatenate([out, out], axis=1)
        o_ref[0, 0] = jnp.where(row < 127, out, 0.0).astype(o_ref.dtype)


def _compress(t16k, t16v, pek, pev, w1k, w1v, w2k, w2v):
    b = t16k.shape[0]
    tspec = pl.BlockSpec((1, 1, 128, 1024), lambda i, g: (i, g, 0, 0))
    ospec = pl.BlockSpec((1, 1, 128, LANES), lambda i, g: (i, g, 0, 0))
    oshape = jax.ShapeDtypeStruct((b, N_KV, 128, LANES), BF16)
    return pl.pallas_call(
        _compress_kernel,
        grid=(b, N_KV),
        in_specs=[tspec, tspec, _resident(pek.shape), _resident(pev.shape), _resident(w1k.shape),
                  _resident(w1v.shape), _resident(w2k.shape), _resident(w2v.shape)],
        out_specs=[ospec, ospec],
        out_shape=[oshape, oshape],
        compiler_params=_cparams(("parallel", "parallel")),
        name="nsa_compress",
    )(t16k, t16v, pek, pev, w1k, w1v, w2k, w2v)


def _split3(x):
    hi = x.astype(BF16)
    r1 = x - hi.astype(F32)
    mid = r1.astype(BF16)
    lo = (r1 - mid.astype(F32)).astype(BF16)
    return hi, mid, lo


def _roll_half(x):
    u = pltpu.bitcast(x, jnp.uint32)
    return pltpu.bitcast(pltpu.roll(u, HEAD_DIM, axis=1), BF16)


def _lane_iota(n):
    return lax.broadcasted_iota(jnp.int32, (n, LANES), 1)


def _unit(lane, at):
    return jnp.where(lane == at, 1.0, 0.0).astype(BF16)


def _q_halves(qt, lo_fill, hi_fill):
    is_lo = _lane_iota(TQ) < HEAD_DIM
    qp0, qp1 = qt[:, :LANES], qt[:, LANES:]
    q_lo = jnp.concatenate([jnp.where(is_lo, qp0, lo_fill), jnp.where(is_lo, qp1, lo_fill)], axis=0)
    q_hi = jnp.concatenate([jnp.where(is_lo, hi_fill, qp0), jnp.where(is_lo, hi_fill, qp1)], axis=0)
    return q_lo, q_hi


def _scores(q_lo, q_hi, k_lo, k_hi):
    return jnp.concatenate([_nt(q_lo, k_lo), _nt(q_hi, k_hi)], axis=0)


def _nsa_kernel(q_ref, kv_ref, kc_ref, vc_ref, gl_ref, bg_ref, biasc_ref, tsel_ref, twin_ref, ovt_ref, o_ref,
                klo, khi, vlo, vhi, kwlo, kwhi, vwlo, vwhi, selq, part, s_scr):
    s_len = q_ref.shape[1]
    nq = s_len // TQ

    kv = kv_ref[0]
    kv01, kv23 = kv[:, :LANES], kv[:, LANES:]
    rot01, rot23 = _roll_half(kv01), _roll_half(kv23)
    lane = _lane_iota(s_len)
    is_lo = lane < HEAD_DIM
    blk = lax.shift_right_logical(lax.broadcasted_iota(jnp.int32, (s_len, LANES), 0), 6)
    zero = jnp.zeros((s_len, LANES), BF16)
    klo[...] = jnp.where(is_lo, kv01, jnp.where(blk == lane - HEAD_DIM, 1.0, 0.0).astype(BF16))
    khi[...] = jnp.where(is_lo, jnp.where(blk == lane, 1.0, 0.0).astype(BF16), rot01)
    vlo[...] = jnp.where(is_lo, rot01, _unit(lane, HEAD_DIM))
    vhi[...] = jnp.where(is_lo, _unit(lane, 0), kv01)
    kwlo[...] = jnp.where(is_lo, kv23, zero)
    kwhi[...] = jnp.where(is_lo, zero, rot23)
    vwlo[...] = jnp.where(is_lo, rot23, _unit(lane, HEAD_DIM))
    vwhi[...] = jnp.where(is_lo, _unit(lane, 0), kv23)

    is_lo_t = _lane_iota(TQ) < HEAD_DIM
    zeros_t = jnp.zeros((TQ, LANES), BF16)
    bg = bg_ref[0]

    def gates_of(r0):
        return jax.nn.sigmoid(gl_ref[0, 0, pl.ds(r0, TQ), :] + bg)

    def pair_rows(pr):
        return slice(pr * TQ, (pr + 1) * TQ)

    def tile_a(i, carry):
        r0 = pl.multiple_of(i * TQ, TQ)
        q_lo, q_hi = _q_halves(q_ref[0, pl.ds(r0, TQ), :], zeros_t, zeros_t)

        kcc = kc_ref[0, 0]
        bias_c = biasc_ref[0, i]
        s = _scores(q_lo, q_hi, kcc, kcc) + bias_c
        valid = bias_c > 0.5 * NEG_INF
        m = jnp.max(s, axis=-1, keepdims=True)
        p = jnp.where(valid, jnp.exp(s - m), 0.0)
        den = jnp.sum(p, axis=-1, keepdims=True)
        p_cmp = p / jnp.maximum(den, 1e-30)
        o_cmp = _dot(p_cmp.astype(BF16), vc_ref[0, 0])

        psum = p_cmp[0:TQ] + p_cmp[TQ:2 * TQ] + p_cmp[2 * TQ:3 * TQ] + p_cmp[3 * TQ:4 * TQ]
        ovt = ovt_ref[...]
        imp_t = sum(_nt(ovt, part_) for part_ in _split3(psum))
        bi = lax.broadcasted_iota(jnp.int32, (32, TQ), 0)
        tok = i * TQ + lax.broadcasted_iota(jnp.int32, (32, TQ), 1)
        cur = lax.shift_right_logical(tok, 6)
        forced = (bi == 0) | (bi == cur) | (bi == cur - 1)
        score = jnp.where(forced, imp_t + FORCE_BONUS, jnp.where(bi <= cur, imp_t, -1.0))
        sel = bi < 0
        for _ in range(SEL_TOPN):
            mx = jnp.max(score, axis=0, keepdims=True)
            idx = jnp.min(jnp.where(score == mx, bi, 32), axis=0, keepdims=True)
            pick = bi == idx
            sel = sel | pick
            score = jnp.where(pick, -3e38, score)
        selneg_t = jnp.where(sel, 0.0, NEG_INF)
        z32 = jnp.zeros((32, TQ), F32)
        selq[i] = jnp.concatenate([selneg_t, z32, selneg_t, z32], axis=0).T.astype(BF16)

        sw = []
        for cc in range(3):
            c = i - 2 + cc
            rc = pl.multiple_of(jnp.maximum(c, 0) * 128, 128)
            sc = _scores(q_lo, q_hi, kwlo[pl.ds(rc, 128), :], kwhi[pl.ds(rc, 128), :])
            sw.append((sc + twin_ref[0, cc] + jnp.where(c >= 0, 0.0, NEG_INF), rc))
        m_w = jnp.max(jnp.maximum(jnp.maximum(sw[0][0], sw[1][0]), sw[2][0]), axis=-1, keepdims=True)
        ow_lo = jnp.zeros((HALF, LANES), F32)
        ow_hi = jnp.zeros((HALF, LANES), F32)
        for sc, rc in sw:
            pc = jnp.exp(sc - m_w).astype(BF16)
            ow_lo = ow_lo + _dot(pc[:HALF], vwlo[pl.ds(rc, 128), :])
            ow_hi = ow_hi + _dot(pc[HALF:], vwhi[pl.ds(rc, 128), :])

        gates = gates_of(r0)
        pairs = []
        for pr in range(2):
            rl, rh = 2 * pr, 2 * pr + 1
            rows = pair_rows(pr)
            w_lo = gates[:, 3 * rl + 2:3 * rl + 3] / ow_lo[rows][:, HEAD_DIM:HEAD_DIM + 1]
            w_hi = gates[:, 3 * rh + 2:3 * rh + 3] / ow_hi[rows][:, 0:1]
            lo = gates[:, 3 * rl:3 * rl + 1] * o_cmp[:HALF][rows] + w_lo * ow_lo[rows]
            hi = gates[:, 3 * rh:3 * rh + 1] * o_cmp[HALF:][rows] + w_hi * ow_hi[rows]
            pairs.append(jnp.where(is_lo_t, lo, hi))
        part[pl.ds(r0, TQ), :] = jnp.concatenate(pairs, axis=1)
        return carry

    lax.fori_loop(0, nq, tile_a, 0, unroll=2)

    def tile_b(i, carry):
        r0 = pl.multiple_of(i * TQ, TQ)
        seln = selq[i]
        q_lo, q_hi = _q_halves(q_ref[0, pl.ds(r0, TQ), :], seln, seln)
        n_super = lax.shift_right_logical(i, 2) + 1

        def pass1(sc_i, macc):
            for cc in range(SUPER):
                c = sc_i * SUPER + cc
                rc = pl.multiple_of(c * 128, 128)
                sc = _scores(q_lo, q_hi, klo[pl.ds(rc, 128), :], khi[pl.ds(rc, 128), :]) + tsel_ref[0, i - c + 3]
                s_scr[c] = sc
                macc = jnp.maximum(macc, sc)
            return macc

        macc = lax.fori_loop(0, n_super, pass1, jnp.full((ROWS, 128), NEG_INF, F32))
        m_sel = jnp.max(macc, axis=-1, keepdims=True)

        def pass2(sc_i, acc):
            a_lo, a_hi = acc
            for cc in range(SUPER):
                c = sc_i * SUPER + cc
                rc = pl.multiple_of(c * 128, 128)
                pc = jnp.exp(s_scr[c] - m_sel).astype(BF16)
                a_lo = a_lo + _dot(pc[:HALF], vlo[pl.ds(rc, 128), :])
                a_hi = a_hi + _dot(pc[HALF:], vhi[pl.ds(rc, 128), :])
            return a_lo, a_hi

        a_lo, a_hi = lax.fori_loop(0, n_super, pass2,
                                   (jnp.zeros((HALF, LANES), F32), jnp.zeros((HALF, LANES), F32)))
        gates = gates_of(r0)
        pairs = []
        for pr in range(2):
            rl, rh = 2 * pr, 2 * pr + 1
            rows = pair_rows(pr)
            w_lo = gates[:, 3 * rl + 1:3 * rl + 2] / jnp.maximum(a_lo[rows][:, HEAD_DIM:HEAD_DIM + 1], 1e-30)
            w_hi = gates[:, 3 * rh + 1:3 * rh + 2] / jnp.maximum(a_hi[rows][:, 0:1], 1e-30)
            pairs.append(jnp.where(is_lo_t, w_lo * a_lo[rows], w_hi * a_hi[rows]))
        o_ref[0, pl.ds(r0, TQ), :] = (jnp.concatenate(pairs, axis=1) + part[pl.ds(r0, TQ), :]).astype(o_ref.dtype)
        return carry

    lax.fori_loop(0, nq, tile_b, 0)


def _nsa_attention(q, kvp, kcmp, vcmp, glg, bgg, biasc, tsel, twin, ovt):
    b, s, _ = q.shape
    nq = s // TQ
    tok_spec = pl.BlockSpec((1, s, GQA_R * HEAD_DIM), lambda g, bi: (bi, 0, g))
    cmp_spec = pl.BlockSpec((1, 1, 128, LANES), lambda g, bi: (bi, g, 0, 0))
    kv_scr = pltpu.VMEM((s, LANES), BF16)
    return pl.pallas_call(
        _nsa_kernel,
        grid=(N_KV, b),
        in_specs=[
            tok_spec, tok_spec, cmp_spec, cmp_spec,
            pl.BlockSpec((1, 1, s, GL_PAD), lambda g, bi: (g, bi, 0, 0)),
            pl.BlockSpec((1, 1, GL_PAD), lambda g, bi: (g, 0, 0)),
            pl.BlockSpec((1, nq, ROWS, 128), lambda g, bi: (g, 0, 0, 0)),
            pl.BlockSpec((1, nq + SUPER - 1, ROWS, 128), lambda g, bi: (g, 0, 0, 0)),
            pl.BlockSpec((1, 3, ROWS, 128), lambda g, bi: (g, 0, 0, 0)),
            _resident(ovt.shape),
        ],
        out_specs=tok_spec,
        out_shape=jax.ShapeDtypeStruct((b, s, D_MODEL), BF16),
        scratch_shapes=[kv_scr] * 8 + [pltpu.VMEM((nq, TQ, LANES), BF16), pltpu.VMEM((s, GQA_R * HEAD_DIM), F32),
                                       pltpu.VMEM((nq, ROWS, 128), F32)],
        compiler_params=_cparams(("parallel", "parallel")),
        name="nsa_attention",
    )(q, kvp, kcmp, vcmp, glg, bgg, biasc, tsel, twin, ovt)


def _dil_kernel(q_ref, kv_ref, tb_ref, o_ref, lse_ref, klo, khi, vlo, vhi, *, nblk):
    l_len = q_ref.shape[1]
    lane = _lane_iota(l_len)
    is_lo = lane < HEAD_DIM
    zero = jnp.zeros((l_len, LANES), BF16)
    lane_t = _lane_iota(TQ)
    is_lo_t = lane_t < HEAD_DIM
    zeros_t = jnp.zeros((TQ, LANES), BF16)

    for g in range(N_KV):
        kvg = kv_ref[0, :, g * LANES:(g + 1) * LANES]
        rot = _roll_half(kvg)
        klo[...] = jnp.where(is_lo, kvg, zero)
        khi[...] = jnp.where(is_lo, zero, rot)
        vlo[...] = jnp.where(is_lo, rot, _unit(lane, HEAD_DIM))
        vhi[...] = jnp.where(is_lo, _unit(lane, 0), kvg)
        cols = slice(g * GQA_R * HEAD_DIM, (g + 1) * GQA_R * HEAD_DIM)

        def tile(i, carry, g=g, cols=cols):
            r0 = pl.multiple_of(i * TQ, TQ)
            q_lo, q_hi = _q_halves(q_ref[0, pl.ds(r0, TQ), cols], zeros_t, zeros_t)
            s1 = _scores(q_lo, q_hi, klo[pl.ds(r0, TQ), :], khi[pl.ds(r0, TQ), :]) + tb_ref[g, 1]
            if nblk > 1:
                rp = pl.multiple_of(jnp.maximum(i - 1, 0) * TQ, TQ)
                s0 = (_scores(q_lo, q_hi, klo[pl.ds(rp, TQ), :], khi[pl.ds(rp, TQ), :]) + tb_ref[g, 0]
                      + jnp.where(i > 0, 0.0, NEG_INF))
                m = jnp.max(jnp.maximum(s0, s1), axis=-1, keepdims=True)
                p0 = jnp.exp(s0 - m).astype(BF16)
                p1 = jnp.exp(s1 - m).astype(BF16)
                a_lo = _dot(p0[:HALF], vlo[pl.ds(rp, TQ), :]) + _dot(p1[:HALF], vlo[pl.ds(r0, TQ), :])
                a_hi = _dot(p0[HALF:], vhi[pl.ds(rp, TQ), :]) + _dot(p1[HALF:], vhi[pl.ds(r0, TQ), :])
            else:
                m = jnp.max(s1, axis=-1, keepdims=True)
                p1 = jnp.exp(s1 - m).astype(BF16)
                a_lo = _dot(p1[:HALF], vlo[pl.ds(r0, TQ), :])
                a_hi = _dot(p1[HALF:], vhi[pl.ds(r0, TQ), :])
            l_lo = a_lo[:, HEAD_DIM:HEAD_DIM + 1]
            l_hi = a_hi[:, 0:1]
            pairs = []
            for pr in range(2):
                rows = slice(pr * TQ, (pr + 1) * TQ)
                pairs.append(jnp.where(is_lo_t, a_lo[rows] / l_lo[rows], a_hi[rows] / l_hi[rows]))
            o_ref[0, pl.ds(r0, TQ), cols] = jnp.concatenate(pairs, axis=1).astype(o_ref.dtype)
            lse = jnp.log(jnp.concatenate([l_lo, l_hi], axis=0)) + m
            lse_t = jnp.zeros((TQ, LANES), F32)
            for kblk, r in enumerate(HEAD_ORDER):
                lse_t = jnp.where(lane_t == g * GQA_R + r, lse[kblk * TQ:(kblk + 1) * TQ], lse_t)
            if g == 0:
                lse_ref[0, pl.ds(r0, TQ), :] = lse_t
            else:
                lse_ref[0, pl.ds(r0, TQ), :] += lse_t
            return carry

        if nblk == 1:
            tile(0, 0)
        else:
            lax.fori_loop(0, nblk, tile, 0)


def _dil_attention(q3, kvp, tb, gi, d):
    b, s, _ = q3.shape
    l = s // d
    n_grp = len(DILATIONS)
    qv = q3.reshape(b, l, d * n_grp * D_MODEL)
    kvv = kvp.reshape(b, l, d * n_grp * 2 * KV_DIM)
    kv_scr = pltpu.VMEM((l, LANES), BF16)
    o, lse = pl.pallas_call(
        functools.partial(_dil_kernel, nblk=l // TQ),
        grid=(b, d),
        in_specs=[pl.BlockSpec((1, l, D_MODEL), lambda bi, r: (bi, 0, r * n_grp + gi)),
                  pl.BlockSpec((1, l, 2 * KV_DIM), lambda bi, r: (bi, 0, r * n_grp + gi)),
                  _resident(tb.shape)],
        out_specs=[pl.BlockSpec((1, l, D_MODEL), lambda bi, r: (bi, 0, r)),
                   pl.BlockSpec((1, l, LANES), lambda bi, r: (bi, 0, r))],
        out_shape=[jax.ShapeDtypeStruct((b, l, d * D_MODEL), BF16), jax.ShapeDtypeStruct((b, l, d * LANES), F32)],
        scratch_shapes=[kv_scr] * 4,
        compiler_params=_cparams(("parallel", "parallel")),
        name="dilated_attention",
    )(qv, kvv, tb)
    return o.reshape(b * s, D_MODEL), lse.reshape(b * s, LANES)


def _post_kernel(*refs, n_grp, final):
    x_ref = refs[0]
    a_refs = refs[1:1 + n_grp]
    pos = 1 + n_grp
    if n_grp > 1:
        lse_refs = refs[pos:pos + n_grp]
        ex_ref = refs[pos + n_grp]
        pos += n_grp + 1
    wo_ref, gf_ref, wup_ref, wdn_ref = refs[pos:pos + 4]
    pos += 4
    if final:
        gfin_ref = refs[pos]
        pos += 1
    o_ref = refs[pos]

    if n_grp == 1:
        a = a_refs[0][...]
    else:
        lses = [r[...] for r in lse_refs]
        mx = functools.reduce(jnp.maximum, lses)
        ws = [jnp.exp(v - mx) for v in lses]
        tot = functools.reduce(lambda u, v: u + v, ws)
        ex = ex_ref[...]
        a32 = jnp.zeros((TM, D_MODEL), F32)
        for w, a_ref in zip(ws, a_refs):
            hi, mid, _ = _split3(w / tot)
            a32 = a32 + (_dot(hi, ex) + _dot(mid, ex)) * a_ref[...].astype(F32)
        a = a32.astype(BF16)
    x1 = x_ref[...] + _dot(a, wo_ref[...])
    h = (x1 * lax.rsqrt(jnp.mean(x1 * x1, axis=-1, keepdims=True) + RMS_EPS) * gf_ref[...]).astype(BF16)
    acts = []
    ch = 256
    for c0 in range(0, FFN_HIDDEN, ch):
        u = _dot(h, wup_ref[:, c0:c0 + ch])
        v = _dot(h, wup_ref[:, FFN_HIDDEN + c0:FFN_HIDDEN + c0 + ch])
        acts.append((jax.nn.silu(u) * v).astype(BF16))
    act = jnp.concatenate(acts, axis=1)
    x2 = x1 + _dot(act, wdn_ref[...])
    if final:
        x2 = x2 * lax.rsqrt(jnp.mean(x2 * x2, axis=-1, keepdims=True) + RMS_EPS) * gfin_ref[...]
    o_ref[...] = x2


def _post(x2d, attn, lses, ex, wo, gf, wup, wdn, gfin):
    m = x2d.shape[0]
    n_grp = len(attn)
    final = gfin is not None
    row = lambda n: pl.BlockSpec((TM, n), lambda i: (i, 0))
    args = [x2d] + list(attn)
    specs = [row(D_MODEL)] + [row(D_MODEL)] * n_grp
    if n_grp > 1:
        args += list(lses) + [ex]
        specs += [row(LANES)] * n_grp + [_resident(ex.shape)]
    args += [wo, gf, wup, wdn]
    specs += [_resident(wo.shape), _resident(gf.shape), _resident(wup.shape), _resident(wdn.shape)]
    if final:
        args.append(gfin)
        specs.append(_resident(gfin.shape))
    return pl.pallas_call(
        functools.partial(_post_kernel, n_grp=n_grp, final=final),
        grid=(m // TM,),
        in_specs=specs,
        out_specs=row(D_MODEL),
        out_shape=jax.ShapeDtypeStruct((m, D_MODEL), F32),
        compiler_params=_cparams(("parallel",)),
        name="outproj_ffn",
    )(*args)


def _t5_bucket(dist):
    max_exact = NUM_BUCKETS // 2
    d = jnp.maximum(dist, 0)
    log_ratio = jnp.log(jnp.maximum(d, max_exact).astype(F32) / max_exact) / math.log(MAX_DISTANCE / max_exact)
    large = max_exact + (log_ratio * (NUM_BUCKETS - max_exact)).astype(jnp.int32)
    return jnp.where(d < max_exact, d, jnp.minimum(large, NUM_BUCKETS - 1))


def _bias_by_dist(rel_bias, dist, valid):
    return jnp.where(valid[None, :], rel_bias[_t5_bucket(dist)].T, NEG_INF).astype(F32)


def _hankel(w, rows, cols):
    lw = w.shape[-1]
    a = jnp.tile(w, (1, rows + 1))[:, :rows * (lw + 1)]
    return a.reshape(w.shape[0], rows, lw + 1)[:, :, :cols]


def _stack_heads(t):
    n = t.shape[1]
    t = t.reshape(N_KV, GQA_R, n, TQ, 128)[:, jnp.array(HEAD_ORDER)]
    return t.transpose(0, 2, 1, 3, 4).reshape(N_KV, n, ROWS, 128)


def _toeplitz_tiles(f_of_dist, lo_tile, n_tiles):
    z = jnp.arange(n_tiles * 128 + 127)
    w = f_of_dist(z + lo_tile * 128 - 127)
    g = _hankel(w, TQ, n_tiles * 128).reshape(N_HEADS, TQ, n_tiles, 128)[..., ::-1]
    return _stack_heads(g.transpose(0, 2, 1, 3))


def kernel(x, rel_bias, norm_mix, norm_ffn, a_w_in, a_b_gate, a_pe_k, a_w1_k, a_w2_k, a_pe_v, a_w1_v, a_w2_v,
           a_w_out, kv_norm, kv_w, b_w_q, b_w_out, ffn_w_up, ffn_w_down, final_norm):
    b, s, _ = x.shape
    m = b * s
    nq = s // TQ
    n_dil = len(DILATIONS)
    scale = HEAD_DIM ** -0.5
    x2d = x.reshape(m, D_MODEL)

    w_in = a_w_in[0]
    n_gl = 3 * GQA_R
    w_q = (w_in[:, :D_MODEL] * scale).astype(BF16)
    w_kvc = w_in[:, D_MODEL:D_MODEL + 2 * KV_DIM].astype(BF16)
    w_ksw = w_in[:, D_MODEL + 2 * KV_DIM:D_MODEL + 6 * KV_DIM].reshape(D_MODEL, 4, N_KV, HEAD_DIM)
    w_ksw = w_ksw.transpose(0, 2, 1, 3).reshape(D_MODEL, 4 * KV_DIM).astype(BF16)
    w_gl = w_in[:, D_MODEL + 6 * KV_DIM:].reshape(D_MODEL, N_KV, n_gl)
    w_gl = jnp.pad(w_gl, ((0, 0), (0, 0), (0, GL_PAD - n_gl))).reshape(D_MODEL, N_KV * GL_PAD)
    w_gl = jnp.pad(w_gl, ((0, 0), (0, LANES - N_KV * GL_PAD))).astype(BF16)
    q, kvc, ksw, gl = _norm_proj(
        x2d, norm_mix[0:1], [w_q, w_kvc, w_ksw, w_gl], [0, 0, 0, 0],
        [(m, D_MODEL), (m, 2 * KV_DIM), (m, 4 * KV_DIM), (N_KV, m, GL_PAD)], [BF16, BF16, BF16, F32])

    def t16(t):
        t = t.reshape(b, s // CMP_STRIDE, CMP_STRIDE, N_KV, HEAD_DIM)
        return t.transpose(0, 3, 1, 2, 4).reshape(b, N_KV, s // CMP_STRIDE, CMP_STRIDE * HEAD_DIM)

    kcmp, vcmp = _compress(
        t16(kvc[:, :KV_DIM]), t16(kvc[:, KV_DIM:]),
        jnp.pad(a_pe_k[0].reshape(1, -1), ((0, 7), (0, 0))), jnp.pad(a_pe_v[0].reshape(1, -1), ((0, 7), (0, 0))),
        a_w1_k[0].astype(BF16), a_w1_v[0].astype(BF16), a_w2_k[0].astype(BF16), a_w2_v[0].astype(BF16))

    bgg = jnp.pad(a_b_gate[0].reshape(N_KV, 1, n_gl), ((0, 0), (0, 0), (0, GL_PAD - n_gl)))

    causal = lambda dist: _bias_by_dist(rel_bias, dist, dist >= 0)
    band = lambda dist: _bias_by_dist(rel_bias, dist, (dist >= 0) & (dist <= WIN - 1))
    tsel = _toeplitz_tiles(causal, -(SUPER - 1), nq + SUPER - 1)
    twin = _toeplitz_tiles(band, 0, 3)[:, ::-1]
    span = CMP_STRIDE * 127
    zc = jnp.arange(nq * 128 + span + 127 + 1)
    g_c = _hankel(causal(zc - span - CMP_BLOCK + 1), TQ, nq * 128 + span)
    biasc = jnp.stack([g_c[:, :, i * 128:i * 128 + span + 1:CMP_STRIDE][..., ::-1] for i in range(nq)], axis=1)
    biasc = _stack_heads(jnp.where(jnp.arange(128) < 127, biasc, NEG_INF))
    ci = jnp.arange(128)[None, :] * CMP_STRIDE
    sj = jnp.arange(s // SEL_BLOCK)[:, None] * SEL_BLOCK
    ovt = ((ci < sj + SEL_BLOCK) & (ci + CMP_BLOCK > sj) & (jnp.arange(128)[None, :] < 127)).astype(BF16)

    attn = _nsa_attention(q.reshape(b, s, D_MODEL), ksw.reshape(b, s, 4 * KV_DIM), kcmp, vcmp,
                          gl.reshape(N_KV, b, s, GL_PAD), bgg, biasc, tsel, twin, ovt)
    x2d = _post(x2d, [attn.reshape(m, D_MODEL)], None, None, a_w_out[0].astype(BF16), norm_ffn[0:1],
                ffn_w_up[0].astype(BF16), ffn_w_down[0].astype(BF16), None)

    gains = jnp.stack([norm_mix[1], kv_norm])
    w_kv = kv_w.reshape(D_MODEL, n_dil, 2, N_KV, HEAD_DIM).transpose(0, 1, 3, 2, 4).reshape(D_MODEL, -1)
    q3, kvp = _norm_proj(x2d, gains, [(b_w_q[0] * scale).astype(BF16), w_kv.astype(BF16)], [0, 1],
                         [(m, n_dil * D_MODEL), (m, n_dil * 2 * KV_DIM)], [BF16, BF16])
    q3 = q3.reshape(b, s, n_dil * D_MODEL)
    kvp = kvp.reshape(b, s, n_dil * 2 * KV_DIM)
    outs, lses = [], []
    for gi, (window, d) in enumerate(DILATIONS):
        span_d = window // d
        dil = lambda dist, d=d, span_d=span_d: _bias_by_dist(rel_bias, dist * d, (dist >= 0) & (dist <= span_d))
        tb = _toeplitz_tiles(dil, 0, 2)[:, ::-1]
        o, lse = _dil_attention(q3, kvp, tb, gi, d)
        outs.append(o)
        lses.append(lse)
    ex = (jnp.arange(D_MODEL)[None, :] // HEAD_DIM == jnp.arange(LANES)[:, None]).astype(BF16)
    out = _post(x2d, outs, lses, ex, b_w_out[0].astype(BF16), norm_ffn[1:2],
                ffn_w_up[1].astype(BF16), ffn_w_down[1].astype(BF16), final_norm.reshape(1, D_MODEL))
    return out.reshape(b, s, D_MODEL)
```

```python
import functools
import math

import jax
import jax.numpy as jnp
from jax import lax
from jax.experimental import pallas as pl
from jax.experimental.pallas import tpu as pltpu

F32 = jnp.float32
BF16 = jnp.bfloat16

D_MODEL = 1024
HEAD_DIM = 64
N_HEADS = 16
N_KV = 4
GQA_R = 4
KV_DIM = N_KV * HEAD_DIM
NUM_BUCKETS = 32
MAX_DISTANCE = 2048
CMP_BLOCK = 32
CMP_STRIDE = 16
SEL_BLOCK = 64
SEL_TOPN = 8
FORCE_BONUS = 100.0
WIN = 256
DILATIONS = ((128, 1), (512, 4), (2048, 16))
FFN_HIDDEN = 2816
RMS_EPS = 1e-6
NEG_INF = -1e30

LANES = 128
TQ = 128
HALF = 2 * TQ
ROWS = GQA_R * TQ
HEAD_ORDER = (0, 2, 1, 3)
KC = 256
SUPER = 4
WIN_KEYS = 3 * TQ
GL_PAD = 16
TM = 512
PERM_BLOCK = 256
VMEM_LIMIT = 56 * 1024 * 1024

NT_DIMS = (((1,), (1,)), ((), ()))


def _cparams(sem):
    return pltpu.CompilerParams(dimension_semantics=sem, vmem_limit_bytes=VMEM_LIMIT)


def _resident(shape):
    nd = len(shape)
    return pl.BlockSpec(shape, lambda *_: (0,) * nd, pipeline_mode=pl.Buffered(1))


def _nt(a, b):
    return lax.dot_general(a, b, NT_DIMS, preferred_element_type=F32)


def _dot(a, b):
    return jnp.dot(a, b, preferred_element_type=F32)


def _norm_proj_kernel(x_ref, g_ref, *refs, gain_ids, perm_ids, n_w, n_p):
    p_refs, w_refs, o_refs = refs[:n_p], refs[n_p:n_p + n_w], refs[n_p + n_w:]
    x = x_ref[...]
    xn = x * lax.rsqrt(jnp.mean(x * x, axis=-1, keepdims=True) + RMS_EPS)
    hs = {}
    for w_ref, o_ref, gid, pid in zip(w_refs, o_refs, gain_ids, perm_ids):
        if (gid, None) not in hs:
            hs[(gid, None)] = (xn * g_ref[gid:gid + 1, :]).astype(BF16)
        if (gid, pid) not in hs:
            hn, pm = hs[(gid, None)], p_refs[pid][...]
            hs[(gid, pid)] = jnp.concatenate(
                [_dot(pm, hn[r:r + PERM_BLOCK]) for r in range(0, TM, PERM_BLOCK)], axis=0).astype(BF16)
        h = hs[(gid, pid)]
        n = w_ref.shape[1]
        if len(o_ref.shape) == 3:
            y = _dot(h, w_ref[...])
            w = o_ref.shape[2]
            for g in range(o_ref.shape[0]):
                o_ref[g] = y[:, g * w:(g + 1) * w].astype(o_ref.dtype)
            continue
        for c0 in range(0, n, 512):
            c1 = min(n, c0 + 512)
            o_ref[:, c0:c1] = _dot(h, w_ref[:, c0:c1]).astype(o_ref.dtype)


def _norm_proj(x2, gains, weights, gain_ids, out_shapes, out_dtypes, perms=(), perm_ids=None):
    m = x2.shape[0]
    n_w = len(weights)
    perm_ids = tuple(perm_ids) if perm_ids is not None else (None,) * n_w
    kern = functools.partial(_norm_proj_kernel, gain_ids=tuple(gain_ids), perm_ids=perm_ids, n_w=n_w,
                             n_p=len(perms))
    out_specs = []
    for shp in out_shapes:
        if len(shp) == 3:
            out_specs.append(pl.BlockSpec((shp[0], TM, shp[2]), lambda i: (0, i, 0)))
        else:
            out_specs.append(pl.BlockSpec((TM, shp[1]), lambda i: (i, 0)))
    return pl.pallas_call(
        kern,
        grid=(m // TM,),
        in_specs=[pl.BlockSpec((TM, D_MODEL), lambda i: (i, 0)), _resident(gains.shape)]
        + [_resident(p.shape) for p in perms] + [_resident(w.shape) for w in weights],
        out_specs=out_specs,
        out_shape=[jax.ShapeDtypeStruct(shp, dt) for shp, dt in zip(out_shapes, out_dtypes)],
        compiler_params=_cparams(("parallel",)),
        name="norm_proj",
    )(x2, gains, *perms, *weights)


def _compress_kernel(tk_ref, tv_ref, pek_ref, pev_ref, w1k_ref, w1v_ref, w2k_ref, w2v_ref, ok_ref, ov_ref):
    half = CMP_STRIDE * HEAD_DIM
    row = lax.broadcasted_iota(jnp.int32, (128, LANES), 0)
    for t_ref, pe_ref, w1_ref, w2_ref, o_ref in ((tk_ref, pek_ref, w1k_ref, w2k_ref, ok_ref),
                                                 (tv_ref, pev_ref, w1v_ref, w2v_ref, ov_ref)):
        t = t_ref[0, 0]
        u = _dot(t, w1_ref[:half, :])
        v = _dot(t, w1_ref[half:, :])
        c = _dot(pe_ref[...].astype(BF16), w1_ref[...])[0:1, :]
        hid = u + pltpu.roll(v, shift=127, axis=0) + c
        out = _dot(jax.nn.gelu(hid).astype(BF16), w2_ref[...])
        out = jnp.concatenate([out, out], axis=1)
        o_ref[0, 0] = jnp.where(row < 127, out, 0.0).astype(o_ref.dtype)


def _compress(t16k, t16v, pek, pev, w1k, w1v, w2k, w2v):
    b = t16k.shape[0]
    tspec = pl.BlockSpec((1, 1, 128, 1024), lambda i, g: (i, g, 0, 0))
    ospec = pl.BlockSpec((1, 1, 128, LANES), lambda i, g: (i, g, 0, 0))
    oshape = jax.ShapeDtypeStruct((b, N_KV, 128, LANES), BF16)
    return pl.pallas_call(
        _compress_kernel,
        grid=(b, N_KV),
        in_specs=[tspec, tspec, _resident(pek.shape), _resident(pev.shape), _resident(w1k.shape),
                  _resident(w1v.shape), _resident(w2k.shape), _resident(w2v.shape)],
        out_specs=[ospec, ospec],
        out_shape=[oshape, oshape],
        compiler_params=_cparams(("parallel", "parallel")),
        name="nsa_compress",
    )(t16k, t16v, pek, pev, w1k, w1v, w2k, w2v)


def _split3(x):
    hi = x.astype(BF16)
    r1 = x - hi.astype(F32)
    mid = r1.astype(BF16)
    lo = (r1 - mid.astype(F32)).astype(BF16)
    return hi, mid, lo


def _roll_half(x):
    u = pltpu.bitcast(x, jnp.uint32)
    return pltpu.bitcast(pltpu.roll(u, HEAD_DIM, axis=1), BF16)


def _lane_iota(n):
    return lax.broadcasted_iota(jnp.int32, (n, LANES), 1)


def _unit(lane, at):
    return jnp.where(lane == at, 1.0, 0.0).astype(BF16)


def _kv_halves(kv, lane):
    is_lo = lane < HEAD_DIM
    rot = _roll_half(kv)
    zero = jnp.zeros_like(kv)
    return (jnp.where(is_lo, kv, zero), jnp.where(is_lo, zero, rot),
            jnp.where(is_lo, rot, _unit(lane, HEAD_DIM)), jnp.where(is_lo, _unit(lane, 0), kv))


def _q_halves(qt, lo_fill, hi_fill):
    is_lo = _lane_iota(TQ) < HEAD_DIM
    qp0, qp1 = qt[:, :LANES], qt[:, LANES:]
    q_lo = jnp.concatenate([jnp.where(is_lo, qp0, lo_fill), jnp.where(is_lo, qp1, lo_fill)], axis=0)
    q_hi = jnp.concatenate([jnp.where(is_lo, hi_fill, qp0), jnp.where(is_lo, hi_fill, qp1)], axis=0)
    return q_lo, q_hi


def _scores(q_lo, q_hi, k_lo, k_hi):
    return jnp.concatenate([_nt(q_lo, k_lo), _nt(q_hi, k_hi)], axis=0)


def _pair_tiles(a_lo, a_hi, w_lo, w_hi):
    is_lo_t = _lane_iota(TQ) < HEAD_DIM
    out = []
    for pr in range(2):
        rows = slice(pr * TQ, (pr + 1) * TQ)
        out.append(jnp.where(is_lo_t, w_lo[rows] * a_lo[rows], w_hi[rows] * a_hi[rows]))
    return out


def _nsa_kernel(q_ref, kv_ref, kc_ref, vc_ref, gl_ref, bg_ref, biasc_ref, tsel_ref, twin_ref, ovt_ref, o_ref,
                klo, khi, vlo, vhi, kwlo, kwhi, vwlo, vwhi, selq, part, s_scr):
    s_len = q_ref.shape[1]
    nq = s_len // TQ

    kv = kv_ref[0]
    lane = _lane_iota(s_len)
    is_lo = lane < HEAD_DIM
    blk = lax.shift_right_logical(lax.broadcasted_iota(jnp.int32, (s_len, LANES), 0), 6)
    k_lo, k_hi, v_lo, v_hi = _kv_halves(kv[:, :LANES], lane)
    klo[...] = jnp.where(is_lo, k_lo, jnp.where(blk == lane - HEAD_DIM, 1.0, 0.0).astype(BF16))
    khi[...] = jnp.where(is_lo, jnp.where(blk == lane, 1.0, 0.0).astype(BF16), k_hi)
    vlo[...] = v_lo
    vhi[...] = v_hi
    kwlo[...], kwhi[...], vwlo[...], vwhi[...] = _kv_halves(kv[:, LANES:], lane)

    zeros_t = jnp.zeros((TQ, LANES), BF16)
    bg = bg_ref[0]

    def gates_of(r0):
        return jax.nn.sigmoid(gl_ref[0, 0, pl.ds(r0, TQ), :] + bg)

    def gate_cols(gates, branch):
        col = lambda r: gates[:, 3 * r + branch:3 * r + branch + 1]
        return jnp.concatenate([col(0), col(2)], axis=0), jnp.concatenate([col(1), col(3)], axis=0)

    def tile_a(i, carry):
        r0 = pl.multiple_of(i * TQ, TQ)
        q_lo, q_hi = _q_halves(q_ref[0, pl.ds(r0, TQ), :], zeros_t, zeros_t)

        kcc = kc_ref[0, 0]
        bias_c = biasc_ref[0, i]
        s = _scores(q_lo, q_hi, kcc, kcc) + bias_c
        valid = bias_c > 0.5 * NEG_INF
        m = jnp.max(s, axis=-1, keepdims=True)
        p = jnp.where(valid, jnp.exp(s - m), 0.0)
        den = jnp.sum(p, axis=-1, keepdims=True)
        p_cmp = p / jnp.maximum(den, 1e-30)
        o_cmp = _dot(p_cmp.astype(BF16), vc_ref[0, 0])

        psum = p_cmp[0:TQ] + p_cmp[TQ:2 * TQ] + p_cmp[2 * TQ:3 * TQ] + p_cmp[3 * TQ:4 * TQ]
        ovt = ovt_ref[...]
        imp_t = sum(_nt(ovt, part_) for part_ in _split3(psum))
        bi = lax.broadcasted_iota(jnp.int32, (32, TQ), 0)
        tok = i * TQ + lax.broadcasted_iota(jnp.int32, (32, TQ), 1)
        cur = lax.shift_right_logical(tok, 6)
        forced = (bi == 0) | (bi == cur) | (bi == cur - 1)
        score = jnp.where(forced, imp_t + FORCE_BONUS, jnp.where(bi <= cur, imp_t, -1.0))
        sel = bi < 0
        for _ in range(SEL_TOPN):
            mx = jnp.max(score, axis=0, keepdims=True)
            idx = jnp.min(jnp.where(score == mx, bi, 32), axis=0, keepdims=True)
            pick = bi == idx
            sel = sel | pick
            score = jnp.where(pick, -3e38, score)
        selneg_t = jnp.where(sel, 0.0, NEG_INF)
        z32 = jnp.zeros((32, TQ), F32)
        selq[i] = jnp.concatenate([selneg_t, z32, selneg_t, z32], axis=0).T.astype(BF16)

        rs = pl.multiple_of(jnp.maximum(i - 2, 0) * TQ, TQ)
        sw = (_scores(q_lo, q_hi, kwlo[pl.ds(rs, WIN_KEYS), :], kwhi[pl.ds(rs, WIN_KEYS), :])
              + twin_ref[0, jnp.minimum(i, 2)])
        pw = jnp.exp(sw - jnp.max(sw, axis=-1, keepdims=True)).astype(BF16)
        ow_lo = _dot(pw[:HALF], vwlo[pl.ds(rs, WIN_KEYS), :])
        ow_hi = _dot(pw[HALF:], vwhi[pl.ds(rs, WIN_KEYS), :])

        gates = gates_of(r0)
        gc_lo, gc_hi = gate_cols(gates, 0)
        gw_lo, gw_hi = gate_cols(gates, 2)
        cmp_t = _pair_tiles(o_cmp[:HALF], o_cmp[HALF:], gc_lo, gc_hi)
        win_t = _pair_tiles(ow_lo, ow_hi, gw_lo / ow_lo[:, HEAD_DIM:HEAD_DIM + 1], gw_hi / ow_hi[:, 0:1])
        part[pl.ds(r0, TQ), :] = jnp.concatenate([cmp_t[0] + win_t[0], cmp_t[1] + win_t[1]], axis=1)
        return carry

    lax.fori_loop(0, nq, tile_a, 0, unroll=2)

    def tile_b(i, carry):
        r0 = pl.multiple_of(i * TQ, TQ)
        seln = selq[i]
        q_lo, q_hi = _q_halves(q_ref[0, pl.ds(r0, TQ), :], seln, seln)
        n_super = lax.shift_right_logical(i, 2) + 1

        def chunk(sc_i, cc):
            c2 = sc_i * (SUPER // 2) + cc
            return c2, pl.multiple_of(c2 * KC, KC)

        def pass1(sc_i, macc):
            for cc in range(SUPER // 2):
                c2, rc = chunk(sc_i, cc)
                e = i - 2 * c2 + SUPER - 1
                sc = (_scores(q_lo, q_hi, klo[pl.ds(rc, KC), :], khi[pl.ds(rc, KC), :])
                      + jnp.concatenate([tsel_ref[0, e], tsel_ref[0, e - 1]], axis=1))
                s_scr[c2] = sc
                macc = jnp.maximum(macc, jnp.maximum(sc[:, :LANES], sc[:, LANES:]))
            return macc

        macc = lax.fori_loop(0, n_super, pass1, jnp.full((ROWS, LANES), NEG_INF, F32))
        m_sel = jnp.max(macc, axis=-1, keepdims=True)

        def pass2(sc_i, acc):
            a_lo, a_hi = acc
            for cc in range(SUPER // 2):
                c2, rc = chunk(sc_i, cc)
                pc = jnp.exp(s_scr[c2] - m_sel).astype(BF16)
                a_lo = a_lo + _dot(pc[:HALF], vlo[pl.ds(rc, KC), :])
                a_hi = a_hi + _dot(pc[HALF:], vhi[pl.ds(rc, KC), :])
            return a_lo, a_hi

        a_lo, a_hi = lax.fori_loop(0, n_super, pass2,
                                   (jnp.zeros((HALF, LANES), F32), jnp.zeros((HALF, LANES), F32)))
        gs_lo, gs_hi = gate_cols(gates_of(r0), 1)
        sel_t = _pair_tiles(a_lo, a_hi, gs_lo / jnp.maximum(a_lo[:, HEAD_DIM:HEAD_DIM + 1], 1e-30),
                            gs_hi / jnp.maximum(a_hi[:, 0:1], 1e-30))
        o_ref[0, pl.ds(r0, TQ), :] = (jnp.concatenate(sel_t, axis=1) + part[pl.ds(r0, TQ), :]).astype(o_ref.dtype)
        return carry

    lax.fori_loop(0, nq, tile_b, 0)


def _nsa_attention(q, kvp, kcmp, vcmp, glg, bgg, biasc, tsel, twin, ovt):
    b, s, _ = q.shape
    nq = s // TQ
    tok_spec = pl.BlockSpec((1, s, GQA_R * HEAD_DIM), lambda g, bi: (bi, 0, g))
    cmp_spec = pl.BlockSpec((1, 1, 128, LANES), lambda g, bi: (bi, g, 0, 0))
    kv_scr = pltpu.VMEM((s, LANES), BF16)
    return pl.pallas_call(
        _nsa_kernel,
        grid=(N_KV, b),
        in_specs=[
            tok_spec, tok_spec, cmp_spec, cmp_spec,
            pl.BlockSpec((1, 1, s, GL_PAD), lambda g, bi: (g, bi, 0, 0)),
            pl.BlockSpec((1, 1, GL_PAD), lambda g, bi: (g, 0, 0)),
            pl.BlockSpec((1, nq, ROWS, 128), lambda g, bi: (g, 0, 0, 0)),
            pl.BlockSpec((1, nq + SUPER - 1, ROWS, 128), lambda g, bi: (g, 0, 0, 0)),
            pl.BlockSpec((1, 3, ROWS, WIN_KEYS), lambda g, bi: (g, 0, 0, 0)),
            _resident(ovt.shape),
        ],
        out_specs=tok_spec,
        out_shape=jax.ShapeDtypeStruct((b, s, D_MODEL), BF16),
        scratch_shapes=[kv_scr] * 8 + [pltpu.VMEM((nq, TQ, LANES), BF16), pltpu.VMEM((s, GQA_R * HEAD_DIM), F32),
                                       pltpu.VMEM((s // KC, ROWS, KC), F32)],
        compiler_params=_cparams(("parallel", "parallel")),
        name="nsa_attention",
    )(q, kvp, kcmp, vcmp, glg, bgg, biasc, tsel, twin, ovt)


def _dil_kernel(q_ref, kv_ref, tb_ref, o_ref, lse_ref, klo, khi, vlo, vhi):
    nb, d, rp = q_ref.shape[1:4]
    l_len = nb * rp
    nblk = l_len // TQ
    lane = _lane_iota(l_len)
    lane_t = _lane_iota(TQ)
    zeros_t = jnp.zeros((TQ, LANES), BF16)
    bpt = TQ // rp if rp < TQ else 1
    tpb = rp // TQ if rp > TQ else 1

    def rows_of(ref, res, i, cols):
        if rp < TQ:
            return (0, pl.ds(i * bpt, bpt), res, slice(None), cols)
        if rp == TQ:
            return (0, i, res, slice(None), cols)
        return (0, i // tpb, res, pl.ds(pl.multiple_of((i % tpb) * TQ, TQ), TQ), cols)

    def lse_tile(lse, g, base):
        for kblk, r in enumerate(HEAD_ORDER):
            base = jnp.where(lane_t == g * GQA_R + r, lse[kblk * TQ:(kblk + 1) * TQ], base)
        return base

    def finish(a_lo, a_hi, m):
        l_lo, l_hi = a_lo[:, HEAD_DIM:HEAD_DIM + 1], a_hi[:, 0:1]
        o_t = jnp.concatenate(_pair_tiles(a_lo, a_hi, 1.0 / l_lo, 1.0 / l_hi), axis=1)
        return o_t, jnp.log(jnp.concatenate([l_lo, l_hi], axis=0)) + m

    def residue(res, carry):
        if nblk == 1:
            lse_t = jnp.zeros((TQ, LANES), F32)
            for g in range(N_KV):
                cols = slice(g * GQA_R * HEAD_DIM, (g + 1) * GQA_R * HEAD_DIM)
                kvg = kv_ref[0, :, res, :, g * LANES:(g + 1) * LANES].reshape(l_len, LANES)
                k_lo, k_hi, v_lo, v_hi = _kv_halves(kvg, lane)
                qt = q_ref[rows_of(q_ref, res, 0, cols)].reshape(TQ, GQA_R * HEAD_DIM)
                q_lo, q_hi = _q_halves(qt, zeros_t, zeros_t)
                s = _scores(q_lo, q_hi, k_lo, k_hi) + tb_ref[g, 0][:, :TQ]
                m = jnp.max(s, axis=-1, keepdims=True)
                p = jnp.exp(s - m).astype(BF16)
                o_t, lse = finish(_dot(p[:HALF], v_lo), _dot(p[HALF:], v_hi), m)
                o_ref[rows_of(o_ref, res, 0, cols)] = o_t.reshape(bpt, rp, -1).astype(o_ref.dtype)
                lse_t = lse_tile(lse, g, lse_t)
            lse_ref[rows_of(lse_ref, res, 0, slice(None))] = lse_t.reshape(bpt, rp, LANES)
            return carry

        for g in range(N_KV):
            cols = slice(g * GQA_R * HEAD_DIM, (g + 1) * GQA_R * HEAD_DIM)
            kvg = kv_ref[0, :, res, :, g * LANES:(g + 1) * LANES].reshape(l_len, LANES)
            klo[g], khi[g], vlo[g], vhi[g] = _kv_halves(kvg, lane)

            def tile(i, c, g=g, cols=cols):
                rs = pl.multiple_of(jnp.maximum(i - 1, 0) * TQ, TQ)
                qt = q_ref[rows_of(q_ref, res, i, cols)].reshape(TQ, GQA_R * HEAD_DIM)
                q_lo, q_hi = _q_halves(qt, zeros_t, zeros_t)
                s = (_scores(q_lo, q_hi, klo[g, pl.ds(rs, 2 * TQ), :], khi[g, pl.ds(rs, 2 * TQ), :])
                     + tb_ref[g, jnp.minimum(i, 1)])
                m = jnp.max(s, axis=-1, keepdims=True)
                p = jnp.exp(s - m).astype(BF16)
                o_t, lse = finish(_dot(p[:HALF], vlo[g, pl.ds(rs, 2 * TQ), :]),
                                  _dot(p[HALF:], vhi[g, pl.ds(rs, 2 * TQ), :]), m)
                o_ref[rows_of(o_ref, res, i, cols)] = o_t.reshape((bpt, rp, -1) if rp < TQ else (TQ, -1)).astype(
                    o_ref.dtype)
                idx = rows_of(lse_ref, res, i, slice(None))
                shp = (bpt, rp, LANES) if rp < TQ else (TQ, LANES)
                if g == 0:
                    lse_ref[idx] = lse_tile(lse, g, jnp.zeros((TQ, LANES), F32)).reshape(shp)
                else:
                    lse_ref[idx] = lse_tile(lse, g, lse_ref[idx].reshape(TQ, LANES)).reshape(shp)
                return c

            lax.fori_loop(0, nblk, tile, 0, unroll=2)
        return carry

    if d == 1:
        residue(0, 0)
    else:
        lax.fori_loop(0, d, residue, 0)


def _dil_attention(qg, kvg, tb, d):
    b, s, _ = qg.shape
    nb, rp = s // PERM_BLOCK, PERM_BLOCK // d
    l = s // d
    view = lambda t: t.reshape(b, nb, d, rp, t.shape[-1])
    spec = lambda c: pl.BlockSpec((1, nb, d, rp, c), lambda bi: (bi, 0, 0, 0, 0))
    n_scr = N_KV if l > TQ else 1
    kv_scr = pltpu.VMEM((n_scr, l, LANES), BF16)
    o, lse = pl.pallas_call(
        _dil_kernel,
        grid=(b,),
        in_specs=[spec(D_MODEL), spec(2 * KV_DIM), _resident(tb.shape)],
        out_specs=[spec(D_MODEL), spec(LANES)],
        out_shape=[jax.ShapeDtypeStruct((b, nb, d, rp, D_MODEL), BF16),
                   jax.ShapeDtypeStruct((b, nb, d, rp, LANES), F32)],
        scratch_shapes=[kv_scr] * 4,
        compiler_params=_cparams(("parallel",)),
        name="dilated_attention",
    )(view(qg), view(kvg), tb)
    return o.reshape(b * s, D_MODEL), lse.reshape(b * s, LANES)


def _unpermute(pt, x):
    return jnp.concatenate([_dot(pt, x[r:r + PERM_BLOCK]) for r in range(0, TM, PERM_BLOCK)], axis=0)


def _post_kernel(*refs, n_grp, final):
    x_ref = refs[0]
    a_refs = refs[1:1 + n_grp]
    pos = 1 + n_grp
    if n_grp > 1:
        lse_refs = refs[pos:pos + n_grp]
        pt_refs = refs[pos + n_grp:pos + 2 * n_grp - 1]
        ex_ref = refs[pos + 2 * n_grp - 1]
        pos += 2 * n_grp
    wo_ref, gf_ref, wup_ref, wdn_ref = refs[pos:pos + 4]
    pos += 4
    if final:
        gfin_ref = refs[pos]
        pos += 1
    o_ref = refs[pos]

    if n_grp == 1:
        a = a_refs[0][...]
    else:
        outs, lses = [a_refs[0][...].astype(F32)], [lse_refs[0][...]]
        for a_ref, lse_ref, pt_ref in zip(a_refs[1:], lse_refs[1:], pt_refs):
            pt = pt_ref[...]
            outs.append(_unpermute(pt, a_ref[...]))
            lses.append(sum(_unpermute(pt, part_) for part_ in _split3(lse_ref[...])))
        mx = functools.reduce(jnp.maximum, lses)
        ws = [jnp.exp(v - mx) for v in lses]
        tot = functools.reduce(lambda u, v: u + v, ws)
        ex = ex_ref[...]
        a32 = jnp.zeros((TM, D_MODEL), F32)
        for w, o_g in zip(ws, outs):
            hi, mid, _ = _split3(w / tot)
            a32 = a32 + (_dot(hi, ex) + _dot(mid, ex)) * o_g
        a = a32.astype(BF16)
    x1 = x_ref[...] + _dot(a, wo_ref[...])
    h = (x1 * lax.rsqrt(jnp.mean(x1 * x1, axis=-1, keepdims=True) + RMS_EPS) * gf_ref[...]).astype(BF16)
    acts = []
    ch = 256
    for c0 in range(0, FFN_HIDDEN, ch):
        u = _dot(h, wup_ref[:, c0:c0 + ch])
        v = _dot(h, wup_ref[:, FFN_HIDDEN + c0:FFN_HIDDEN + c0 + ch])
        acts.append((jax.nn.silu(u) * v).astype(BF16))
    act = jnp.concatenate(acts, axis=1)
    x2 = x1 + _dot(act, wdn_ref[...])
    if final:
        x2 = x2 * lax.rsqrt(jnp.mean(x2 * x2, axis=-1, keepdims=True) + RMS_EPS) * gfin_ref[...]
    o_ref[...] = x2


def _post(x2d, attn, lses, pts, ex, wo, gf, wup, wdn, gfin):
    m = x2d.shape[0]
    n_grp = len(attn)
    final = gfin is not None
    row = lambda n: pl.BlockSpec((TM, n), lambda i: (i, 0))
    args = [x2d] + list(attn)
    specs = [row(D_MODEL)] + [row(D_MODEL)] * n_grp
    if n_grp > 1:
        args += list(lses) + list(pts) + [ex]
        specs += [row(LANES)] * n_grp + [_resident(p.shape) for p in pts] + [_resident(ex.shape)]
    args += [wo, gf, wup, wdn]
    specs += [_resident(wo.shape), _resident(gf.shape), _resident(wup.shape), _resident(wdn.shape)]
    if final:
        args.append(gfin)
        specs.append(_resident(gfin.shape))
    return pl.pallas_call(
        functools.partial(_post_kernel, n_grp=n_grp, final=final),
        grid=(m // TM,),
        in_specs=specs,
        out_specs=row(D_MODEL),
        out_shape=jax.ShapeDtypeStruct((m, D_MODEL), F32),
        compiler_params=_cparams(("parallel",)),
        name="outproj_ffn",
    )(*args)


def _t5_bucket(dist):
    max_exact = NUM_BUCKETS // 2
    d = jnp.maximum(dist, 0)
    log_ratio = jnp.log(jnp.maximum(d, max_exact).astype(F32) / max_exact) / math.log(MAX_DISTANCE / max_exact)
    large = max_exact + (log_ratio * (NUM_BUCKETS - max_exact)).astype(jnp.int32)
    return jnp.where(d < max_exact, d, jnp.minimum(large, NUM_BUCKETS - 1))


def _bias_by_dist(rel_bias, dist, valid):
    return jnp.where(valid[None, :], rel_bias[_t5_bucket(dist)].T, NEG_INF).astype(F32)


def _hankel(w, rows, cols):
    lw = w.shape[-1]
    a = jnp.tile(w, (1, rows + 1))[:, :rows * (lw + 1)]
    return a.reshape(w.shape[0], rows, lw + 1)[:, :, :cols]


def _stack_heads(t):
    n = t.shape[1]
    t = t.reshape(N_KV, GQA_R, n, TQ, 128)[:, jnp.array(HEAD_ORDER)]
    return t.transpose(0, 2, 1, 3, 4).reshape(N_KV, n, ROWS, 128)


def _toeplitz_tiles(f_of_dist, lo_tile, n_tiles):
    z = jnp.arange(n_tiles * 128 + 127)
    w = f_of_dist(z + lo_tile * 128 - 127)
    g = _hankel(w, TQ, n_tiles * 128).reshape(N_HEADS, TQ, n_tiles, 128)[..., ::-1]
    return _stack_heads(g.transpose(0, 2, 1, 3))


def _window_variants(f_of_dist, n_pos):
    tiles = _toeplitz_tiles(f_of_dist, -(n_pos - 1), 2 * n_pos - 1)
    return jnp.stack([jnp.concatenate([tiles[:, v - p + n_pos - 1] for p in range(n_pos)], axis=-1)
                      for v in range(n_pos)], axis=1)


def _residue_perm(d):
    new = jnp.arange(PERM_BLOCK)
    old = (new % (PERM_BLOCK // d)) * d + new // (PERM_BLOCK // d)
    return (old[:, None] == jnp.arange(PERM_BLOCK)[None, :]).astype(BF16)


def kernel(x, rel_bias, norm_mix, norm_ffn, a_w_in, a_b_gate, a_pe_k, a_w1_k, a_w2_k, a_pe_v, a_w1_v, a_w2_v,
           a_w_out, kv_norm, kv_w, b_w_q, b_w_out, ffn_w_up, ffn_w_down, final_norm):
    b, s, _ = x.shape
    m = b * s
    nq = s // TQ
    n_dil = len(DILATIONS)
    scale = HEAD_DIM ** -0.5
    x2d = x.reshape(m, D_MODEL)

    w_in = a_w_in[0]
    n_gl = 3 * GQA_R
    w_q = (w_in[:, :D_MODEL] * scale).astype(BF16)
    w_kvc = w_in[:, D_MODEL:D_MODEL + 2 * KV_DIM].astype(BF16)
    w_ksw = w_in[:, D_MODEL + 2 * KV_DIM:D_MODEL + 6 * KV_DIM].reshape(D_MODEL, 4, N_KV, HEAD_DIM)
    w_ksw = w_ksw.transpose(0, 2, 1, 3).reshape(D_MODEL, 4 * KV_DIM).astype(BF16)
    w_gl = w_in[:, D_MODEL + 6 * KV_DIM:].reshape(D_MODEL, N_KV, n_gl)
    w_gl = jnp.pad(w_gl, ((0, 0), (0, 0), (0, GL_PAD - n_gl))).reshape(D_MODEL, N_KV * GL_PAD)
    w_gl = jnp.pad(w_gl, ((0, 0), (0, LANES - N_KV * GL_PAD))).astype(BF16)
    q, kvc, ksw, gl = _norm_proj(
        x2d, norm_mix[0:1], [w_q, w_kvc, w_ksw, w_gl], [0, 0, 0, 0],
        [(m, D_MODEL), (m, 2 * KV_DIM), (m, 4 * KV_DIM), (N_KV, m, GL_PAD)], [BF16, BF16, BF16, F32])

    def t16(t):
        t = t.reshape(b, s // CMP_STRIDE, CMP_STRIDE, N_KV, HEAD_DIM)
        return t.transpose(0, 3, 1, 2, 4).reshape(b, N_KV, s // CMP_STRIDE, CMP_STRIDE * HEAD_DIM)

    kcmp, vcmp = _compress(
        t16(kvc[:, :KV_DIM]), t16(kvc[:, KV_DIM:]),
        jnp.pad(a_pe_k[0].reshape(1, -1), ((0, 7), (0, 0))), jnp.pad(a_pe_v[0].reshape(1, -1), ((0, 7), (0, 0))),
        a_w1_k[0].astype(BF16), a_w1_v[0].astype(BF16), a_w2_k[0].astype(BF16), a_w2_v[0].astype(BF16))

    bgg = jnp.pad(a_b_gate[0].reshape(N_KV, 1, n_gl), ((0, 0), (0, 0), (0, GL_PAD - n_gl)))

    causal = lambda dist: _bias_by_dist(rel_bias, dist, dist >= 0)
    band = lambda dist: _bias_by_dist(rel_bias, dist, (dist >= 0) & (dist <= WIN - 1))
    tsel = _toeplitz_tiles(causal, -(SUPER - 1), nq + SUPER - 1)
    twin = _window_variants(band, 3)
    span = CMP_STRIDE * 127
    zc = jnp.arange(nq * 128 + span + 127 + 1)
    g_c = _hankel(causal(zc - span - CMP_BLOCK + 1), TQ, nq * 128 + span)
    biasc = jnp.stack([g_c[:, :, i * 128:i * 128 + span + 1:CMP_STRIDE][..., ::-1] for i in range(nq)], axis=1)
    biasc = _stack_heads(jnp.where(jnp.arange(128) < 127, biasc, NEG_INF))
    ci = jnp.arange(128)[None, :] * CMP_STRIDE
    sj = jnp.arange(s // SEL_BLOCK)[:, None] * SEL_BLOCK
    ovt = ((ci < sj + SEL_BLOCK) & (ci + CMP_BLOCK > sj) & (jnp.arange(128)[None, :] < 127)).astype(BF16)

    attn = _nsa_attention(q.reshape(b, s, D_MODEL), ksw.reshape(b, s, 4 * KV_DIM), kcmp, vcmp,
                          gl.reshape(N_KV, b, s, GL_PAD), bgg, biasc, tsel, twin, ovt)
    x2d = _post(x2d, [attn.reshape(m, D_MODEL)], None, None, None, a_w_out[0].astype(BF16), norm_ffn[0:1],
                ffn_w_up[0].astype(BF16), ffn_w_down[0].astype(BF16), None)

    gains = jnp.stack([norm_mix[1], kv_norm])
    w_q3 = (b_w_q[0] * scale).astype(BF16)
    w_kv = kv_w.reshape(D_MODEL, n_dil, 2, N_KV, HEAD_DIM).transpose(0, 1, 3, 2, 4).reshape(D_MODEL, -1).astype(BF16)
    perms = [_residue_perm(d) for _, d in DILATIONS[1:]]
    weights, gain_ids, perm_ids, shapes = [], [], [], []
    for gi in range(n_dil):
        weights += [w_q3[:, gi * D_MODEL:(gi + 1) * D_MODEL], w_kv[:, gi * 2 * KV_DIM:(gi + 1) * 2 * KV_DIM]]
        gain_ids += [0, 1]
        perm_ids += [None if gi == 0 else gi - 1] * 2
        shapes += [(m, D_MODEL), (m, 2 * KV_DIM)]
    proj = _norm_proj(x2d, gains, weights, gain_ids, shapes, [BF16] * len(weights), perms, perm_ids)
    outs, lses = [], []
    for gi, (window, d) in enumerate(DILATIONS):
        span_d = window // d
        dil = lambda dist, d=d, span_d=span_d: _bias_by_dist(rel_bias, dist * d, (dist >= 0) & (dist <= span_d))
        tb = _window_variants(dil, 2)
        o, lse = _dil_attention(proj[2 * gi].reshape(b, s, D_MODEL), proj[2 * gi + 1].reshape(b, s, 2 * KV_DIM),
                                tb, d)
        outs.append(o)
        lses.append(lse)
    ex = (jnp.arange(D_MODEL)[None, :] // HEAD_DIM == jnp.arange(LANES)[:, None]).astype(BF16)
    out = _post(x2d, outs, lses, [p.T for p in perms], ex, b_w_out[0].astype(BF16), norm_ffn[1:2],
                ffn_w_up[1].astype(BF16), ffn_w_down[1].astype(BF16), final_norm.reshape(1, D_MODEL))
    return out.reshape(b, s, D_MODEL)
```
